```python
import math
import jax
import jax.numpy as jnp
from jax import lax
import numpy as np

D_MODEL = 1024
BATCH = 16
SEQ = 256
DEPTH = 4
DEC_BATCH = 4
DEC_SEQ = 2048
PAST_LEN = 256

GRID_W = 64
N_HEADS = 4
GLA_DK = 64
GLA_DV = 128
GLA_RANK = 16
GLA_TEMP = 16.0
GLA_CHUNK = 64
CONV_CH = 512
CONV_WIDTH = 31
CONV_PAD = (CONV_WIDTH - 1) // 2
DIFF_DH = 64
DIFF_DV = 128
ROPE_FREQS = DIFF_DH // 4
ROPE_BASE = 10000.0
Q_BLOCK = 128
FFN_HIDDEN = -(-8 * D_MODEL // (3 * 256)) * 256
EPS = 1e-6
N_BRANCH = 3

GLA_QW = N_HEADS * GLA_DK
GLA_VW = N_HEADS * GLA_DV
DIFF_QW = N_HEADS * 2 * DIFF_DH
DIFF_VW = N_HEADS * DIFF_DV
SPLITS = (GLA_QW, GLA_QW, GLA_VW, GLA_VW, 2 * GLA_RANK, 2 * CONV_CH, DIFF_QW, DIFF_QW, DIFF_VW, N_BRANCH * D_MODEL)
SPLIT_IDX = tuple(sum(SPLITS[:i + 1]) for i in range(len(SPLITS) - 1))
IN_WIDTH = sum(SPLITS)
BRANCH_WIDTHS = (GLA_VW, CONV_CH, DIFF_VW)
BRANCH_TOTAL = sum(BRANCH_WIDTHS)

kernel_name = 'hybrid_gla_conformer_diffattn_dit_step'

F32 = jnp.float32


def rmsnorm(x, g):
    xf = x.astype(F32)
    y = xf * lax.rsqrt(jnp.mean(xf * xf, axis=-1, keepdims=True) + EPS)
    return (y * g.astype(F32)).astype(x.dtype)


def layernorm(x, g, b):
    xf = x.astype(F32)
    mu = jnp.mean(xf, axis=-1, keepdims=True)
    var = jnp.mean(jnp.square(xf - mu), axis=-1, keepdims=True)
    y = (xf - mu) * lax.rsqrt(var + EPS)
    return (y * g.astype(F32) + b.astype(F32)).astype(x.dtype)


def axial_rope(n_tokens):
    rows = n_tokens // GRID_W
    row = jnp.repeat(jnp.arange(rows, dtype=F32), GRID_W)
    col = jnp.tile(jnp.arange(GRID_W, dtype=F32), rows)
    inv = ROPE_BASE ** (-jnp.arange(ROPE_FREQS, dtype=F32) / ROPE_FREQS)
    ang = jnp.stack([row[:, None] * inv, col[:, None] * inv], axis=1)
    return jnp.cos(ang), jnp.sin(ang)


def apply_rope(x, cos, sin):
    shp = x.shape
    xr = x.reshape(shp[:-1] + (2, 2, ROPE_FREQS)).astype(F32)
    x1, x2 = xr[..., 0, :], xr[..., 1, :]
    c = cos[:, None, None]
    s = sin[:, None, None]
    out = jnp.stack([x1 * c - x2 * s, x1 * s + x2 * c], axis=-2)
    return out.reshape(shp).astype(x.dtype)


def gla_chunk_scan(q, k, v, log_a, s0):
    B, L, H, _ = q.shape
    n = L // GLA_CHUNK

    def chunks(t):
        return jnp.moveaxis(t.astype(F32).reshape((B, n, GLA_CHUNK) + t.shape[2:]), 1, 0)

    mask = jnp.tril(jnp.ones((GLA_CHUNK, GLA_CHUNK), dtype=bool))[None, :, :, None, None]

    def step(s, inp):
        qc, kc, vc, gc = inp
        b = jnp.cumsum(gc, axis=1)
        inter = jnp.einsum('bchk,bhkv->bchv', qc * jnp.exp(b), s)
        rel = jnp.where(mask, b[:, :, None] - b[:, None, :], -jnp.inf)
        attn = jnp.einsum('bihk,bjhk,bijhk->bhij', qc, kc, jnp.exp(rel))
        intra = jnp.einsum('bhij,bjhv->bihv', attn, vc)
        b_last = b[:, -1]
        s_new = s * jnp.exp(b_last)[..., None] + jnp.einsum('bjhk,bjhv->bhkv', kc * jnp.exp(b_last[:, None] - b), vc)
        return s_new, inter + intra

    s_fin, o = lax.scan(step, s0, (chunks(q), chunks(k), chunks(v), chunks(log_a)))
    o = jnp.moveaxis(o, 0, 1).reshape(B, L, H, -1).astype(v.dtype)
    return o, s_fin


def gla_branch(q, k, v, g, lr, w_a2, b_a, norm_g, s0f, s0b):
    B, L, _ = q.shape
    q = q.reshape(B, L, N_HEADS, GLA_DK) * (GLA_DK ** -0.5)
    k = k.reshape(B, L, N_HEADS, GLA_DK)
    v = v.reshape(B, L, N_HEADS, GLA_DV)
    lr_f, lr_b = jnp.split(lr, 2, axis=-1)
    log_a_f = (jax.nn.log_sigmoid((lr_f @ w_a2[0] + b_a[0]).astype(F32)) / GLA_TEMP).reshape(B, L, N_HEADS, GLA_DK)
    log_a_b = (jax.nn.log_sigmoid((lr_b @ w_a2[1] + b_a[1]).astype(F32)) / GLA_TEMP).reshape(B, L, N_HEADS, GLA_DK)
    o_f, s_f = gla_chunk_scan(q, k, v, log_a_f, s0f)
    flip = lambda t: jnp.flip(t, axis=1)
    o_b, s_b = gla_chunk_scan(flip(q), flip(k), flip(v), flip(log_a_b), s0b)
    o = rmsnorm(o_f + flip(o_b), norm_g) * jax.nn.silu(g.reshape(B, L, N_HEADS, GLA_DV))
    return o.reshape(B, L, GLA_VW), s_f, s_b


def conv_branch(u, w, b, ln_g, ln_b):
    a, gate = jnp.split(u, 2, axis=-1)
    y = a * jax.nn.sigmoid(gate)
    y = lax.conv_general_dilated(y, w[:, None, :], window_strides=(1,), padding=[(CONV_PAD, CONV_PAD)],
                                 dimension_numbers=('NWC', 'WIO', 'NWC'), feature_group_count=CONV_CH) + b
    return jax.nn.silu(layernorm(y, ln_g, ln_b))


def diff_attention(q, k, v, lam):
    B, Lq = q.shape[:2]
    nq = Lq // Q_BLOCK
    qb = jnp.moveaxis(q.reshape((B, nq, Q_BLOCK) + q.shape[2:]), 1, 0)
    scale = DIFF_DH ** -0.5

    def block(qi):
        s = jnp.einsum('bqhtd,bkhtd->bhtqk', qi, k).astype(F32) * scale
        p = jax.nn.softmax(s, axis=-1)
        w = (p[:, :, 0] - lam * p[:, :, 1]).astype(v.dtype)
        return jnp.einsum('bhqk,bkhv->bqhv', w, v)

    o = lax.map(block, qb)
    return jnp.moveaxis(o, 0, 1).reshape(B, Lq, N_HEADS, DIFF_DV)


def trunk_layer(x, cond, l, lw, ctx, rope):
    B, L, _ = x.shape
    mod = (jax.nn.silu(cond) @ lw['w_mod'] + lw['b_mod'])[:, None, :]
    sh1, sc1, g1, sh2, sc2, g2 = jnp.split(mod, 6, axis=-1)
    h = rmsnorm(x, lw['norm1_g']) * (1.0 + sc1) + sh1
    proj = h @ lw['w_in']
    gq, gk, gv, gg, glr, cu, dq, dk, dv, gates = jnp.split(proj, SPLIT_IDX, axis=-1)

    if ctx is None:
        s0f = jnp.zeros((B, N_HEADS, GLA_DK, GLA_DV), F32)
        s0b = s0f
    else:
        s0f = ctx[2][:, 0].astype(F32)
        s0b = ctx[2][:, 1].astype(F32)
    o_gla, s_f, s_b = gla_branch(gq, gk, gv, gg, glr, lw['gla_w_a2'], lw['gla_b_a'], lw['gla_norm_g'], s0f, s0b)

    o_conv = conv_branch(cu, lw['conv_w'], lw['conv_b'], lw['conv_ln_g'], lw['conv_ln_b'])

    dq = dq.reshape(B, L, N_HEADS, 2, DIFF_DH)
    dk = dk.reshape(B, L, N_HEADS, 2, DIFF_DH)
    dv = dv.reshape(B, L, N_HEADS, DIFF_DV)
    lam_init = 0.8 - 0.6 * math.exp(-0.3 * l)
    lq1, lk1, lq2, lk2 = lw['diff_lambda'].astype(F32)
    lam = jnp.exp(jnp.sum(lq1 * lk1)) - jnp.exp(jnp.sum(lq2 * lk2)) + lam_init
    if ctx is None:
        keys, vals = dk, dv
    else:
        dq = apply_rope(dq, rope[0], rope[1])
        dk = apply_rope(dk, rope[0], rope[1])
        keys = jnp.concatenate([dk, ctx[0].astype(dk.dtype)], axis=1)
        vals = jnp.concatenate([dv, ctx[1].astype(dv.dtype)], axis=1)
    o_diff = diff_attention(dq, keys, vals, lam)
    o_diff = (rmsnorm(o_diff, lw['diff_subln_g']) * (1.0 - lam_init)).reshape(B, L, DIFF_VW)

    wb = lw['w_branch']
    ga, gb, gc = jnp.split(jax.nn.sigmoid(gates), N_BRANCH, axis=-1)
    o1 = BRANCH_WIDTHS[0]
    o2 = o1 + BRANCH_WIDTHS[1]
    merged = ga * (o_gla @ wb[:o1]) + gb * (o_conv @ wb[o1:o2]) + gc * (o_diff @ wb[o2:])
    x = x + g1 * (merged @ lw['w_out'])

    h2 = rmsnorm(x, lw['norm2_g']) * (1.0 + sc2) + sh2
    a, b = jnp.split(h2 @ lw['w_ffn_in'], 2, axis=-1)
    x = x + g2 * ((jax.nn.silu(a) * b) @ lw['w_ffn_out'])
    if ctx is None:
        return x, (dk, dv, jnp.stack([s_f, s_b], axis=1).astype(x.dtype))
    return x, None


def setup_inputs(seed: int = 0) -> dict:
    key = jax.random.key(seed)
    ks = jax.random.split(key, 32)
    nrm = lambda k, shape, s: jax.random.normal(k, shape, F32) * s
    ones = lambda k, shape: 1.0 + 0.05 * jax.random.normal(k, shape, F32)
    D = D_MODEL
    return {
        'x_prompt': nrm(ks[0], (BATCH, SEQ, D), 1.0),
        'x_sample': nrm(ks[1], (DEC_BATCH, DEC_SEQ, D), 1.0),
        'cache_diff_k': nrm(ks[2], (DEC_BATCH, DEPTH, PAST_LEN, N_HEADS, 2, DIFF_DH), 1.0),
        'cache_diff_v': nrm(ks[3], (DEC_BATCH, DEPTH, PAST_LEN, N_HEADS, DIFF_DV), 1.0),
        'state_gla': nrm(ks[4], (DEC_BATCH, DEPTH, 2, N_HEADS, GLA_DK, GLA_DV), 0.5),
        'c': nrm(ks[5], (DEC_BATCH, D), 1.0),
        'c_ctx': nrm(ks[6], (D,), 1.0),
        'w_mod': nrm(ks[7], (DEPTH, D, 6 * D), 0.5 * D ** -0.5),
        'b_mod': nrm(ks[8], (DEPTH, 6 * D), 0.02),
        'norm1_g': ones(ks[9], (DEPTH, D)),
        'norm2_g': ones(ks[10], (DEPTH, D)),
        'w_in': nrm(ks[11], (DEPTH, D, IN_WIDTH), D ** -0.5),
        'gla_w_a2': nrm(ks[12], (DEPTH, 2, GLA_RANK, GLA_QW), GLA_RANK ** -0.5),
        'gla_b_a': nrm(ks[13], (DEPTH, 2, GLA_QW), 0.1),
        'gla_norm_g': ones(ks[14], (DEPTH, GLA_DV)),
        'conv_w': nrm(ks[15], (DEPTH, CONV_WIDTH, CONV_CH), CONV_WIDTH ** -0.5),
        'conv_b': nrm(ks[16], (DEPTH, CONV_CH), 0.02),
        'conv_ln_g': ones(ks[17], (DEPTH, CONV_CH)),
        'conv_ln_b': nrm(ks[18], (DEPTH, CONV_CH), 0.02),
        'diff_lambda': nrm(ks[19], (DEPTH, 4, DIFF_DH), 0.1),
        'diff_subln_g': ones(ks[20], (DEPTH, DIFF_DV)),
        'w_branch': nrm(ks[21], (DEPTH, BRANCH_TOTAL, D), (BRANCH_TOTAL // N_BRANCH) ** -0.5),
        'w_out': nrm(ks[22], (DEPTH, D, D), D ** -0.5),
        'w_ffn_in': nrm(ks[23], (DEPTH, D, 2 * FFN_HIDDEN), D ** -0.5),
        'w_ffn_out': nrm(ks[24], (DEPTH, FFN_HIDDEN, D), FFN_HIDDEN ** -0.5),
        'final_norm_g': ones(ks[25], (D,)),
    }


def reference(x_prompt, x_sample, cache_diff_k, cache_diff_v, state_gla, c, c_ctx, w_mod, b_mod,
              norm1_g, norm2_g, w_in, gla_w_a2, gla_b_a, gla_norm_g, conv_w, conv_b, conv_ln_g,
              conv_ln_b, diff_lambda, diff_subln_g, w_branch, w_out, w_ffn_in, w_ffn_out, final_norm_g):
    rope = axial_rope(x_sample.shape[1])
    xp = x_prompt
    xs = x_sample
    new_k, new_v, new_s = [], [], []
    for l in range(DEPTH):
        lw = {
            'w_mod': w_mod[l], 'b_mod': b_mod[l], 'norm1_g': norm1_g[l], 'norm2_g': norm2_g[l],
            'w_in': w_in[l], 'gla_w_a2': gla_w_a2[l], 'gla_b_a': gla_b_a[l], 'gla_norm_g': gla_norm_g[l],
            'conv_w': conv_w[l], 'conv_b': conv_b[l], 'conv_ln_g': conv_ln_g[l], 'conv_ln_b': conv_ln_b[l],
            'diff_lambda': diff_lambda[l], 'diff_subln_g': diff_subln_g[l], 'w_branch': w_branch[l],
            'w_out': w_out[l], 'w_ffn_in': w_ffn_in[l], 'w_ffn_out': w_ffn_out[l],
        }
        xp, (k_l, v_l, s_l) = trunk_layer(xp, c_ctx[None, :], l, lw, None, None)
        new_k.append(k_l)
        new_v.append(v_l)
        new_s.append(s_l)
        xs, _ = trunk_layer(xs, c, l, lw, (cache_diff_k[:, l], cache_diff_v[:, l], state_gla[:, l]), rope)
    y_prompt = rmsnorm(xp, final_norm_g)
    y_sample = rmsnorm(xs, final_norm_g)
    new_diff_k = jnp.stack(new_k, axis=1)
    new_diff_v = jnp.stack(new_v, axis=1)
    new_state_gla = jnp.stack(new_s, axis=1)
    return (y_prompt, y_sample, new_diff_k, new_diff_v, new_state_gla)
```

```python
import functools
import math

import jax
import jax.numpy as jnp
from jax import lax
from jax.experimental import pallas as pl
from jax.experimental.pallas import tpu as pltpu

F32 = jnp.float32
BF16 = jnp.bfloat16

D_MODEL = 1024
N_HEADS = 4
GLA_DK = 64
GLA_DV = 128
GLA_RANK = 16
GLA_TEMP = 16.0
GLA_CHUNK = 64
CONV_CH = 512
CONV_WIDTH = 31
CONV_PAD = (CONV_WIDTH - 1) // 2
DIFF_DH = 64
DIFF_DV = 128
ROPE_FREQS = DIFF_DH // 4
ROPE_BASE = 10000.0
GRID_W = 64
FFN_HIDDEN = -(-8 * D_MODEL // (3 * 256)) * 256
EPS = 1e-6
N_BRANCH = 3

GLA_QW = N_HEADS * GLA_DK
GLA_VW = N_HEADS * GLA_DV
DIFF_QW = N_HEADS * 2 * DIFF_DH
DIFF_VW = N_HEADS * DIFF_DV
SPLITS = (GLA_QW, GLA_QW, GLA_VW, GLA_VW, 2 * GLA_RANK, 2 * CONV_CH, DIFF_QW, DIFF_QW, DIFF_VW, N_BRANCH * D_MODEL)
OFFS = tuple(sum(SPLITS[:i]) for i in range(len(SPLITS) + 1))

TOKEN_BLOCK = 256
LANES = 128
HALO = 16
COND_ROWS = 8
LR_PAD = LANES
FFN_CHUNK = 256
VMEM_LIMIT = 56 * 1024 * 1024

C_Q, C_K, C_V, C_G = 0, GLA_QW, 2 * GLA_QW, 2 * GLA_QW + GLA_VW
C_CU = C_G + GLA_VW
C_DQ = C_CU + 2 * CONV_CH
C_DK = C_DQ + DIFF_QW
C_DV = C_DK + DIFF_QW
C_GT = C_DV + DIFF_VW
C_LR = C_GT + N_BRANCH * D_MODEL
IN_PACKED = C_LR + LR_PAD


def _params(n_axes=1):
    return pltpu.CompilerParams(dimension_semantics=("arbitrary",) * n_axes, vmem_limit_bytes=VMEM_LIMIT)


def _resident(shape, index_map):
    return pl.BlockSpec(shape, index_map, pipeline_mode=pl.Buffered(1))


def _silu(x):
    return x * jax.nn.sigmoid(x)


def _dot(a, b):
    return jnp.dot(a, b, preferred_element_type=F32)


def _dot_nt(a, b):
    return lax.dot_general(a, b, (((1,), (1,)), ((), ())), preferred_element_type=F32)


def _dot_tn(a, b):
    return lax.dot_general(a, b, (((0,), (0,)), ((), ())), preferred_element_type=F32)


class _Blocks:
    def __init__(self, n_pb, n_ss, bps):
        self.n_pb, self.n_ss, self.bps = n_pb, n_ss, bps
        self.n_blocks = n_pb + n_ss * bps
        self.n_seq = n_pb + n_ss

    def is_prompt(self, t):
        return t < self.n_pb

    def cond_row(self, t):
        return jnp.where(t < self.n_pb, 0, 1 + (t - self.n_pb) // self.bps)

    def seq(self, t):
        return jnp.where(t < self.n_pb, t, self.n_pb + (t - self.n_pb) // self.bps)

    def pos(self, t):
        return jnp.where(t < self.n_pb, 0, (t - self.n_pb) % self.bps)

    def last_pos(self, t):
        return jnp.where(t < self.n_pb, 0, self.bps - 1)

    def rev(self, t):
        u = t - self.n_pb
        return jnp.where(t < self.n_pb, t, self.n_pb + (u // self.bps) * self.bps + (self.bps - 1 - u % self.bps))


def _mod_kernel(c_ref, w_ref, b_ref, o_ref):
    s = _silu(c_ref[...]).astype(BF16)
    o_ref[...] = _dot(s, w_ref[...].astype(BF16)) + b_ref[...]


def _modulation(cond, w_mod, b_mod):
    depth = w_mod.shape[0]
    tn = 1536
    return pl.pallas_call(
        _mod_kernel,
        grid=(depth, 6 * D_MODEL // tn),
        in_specs=[
            pl.BlockSpec((COND_ROWS, D_MODEL), lambda l, j: (0, 0)),
            pl.BlockSpec((None, D_MODEL, tn), lambda l, j: (l, 0, j)),
            pl.BlockSpec((None, 1, tn), lambda l, j: (l, 0, j)),
        ],
        out_specs=pl.BlockSpec((None, COND_ROWS, tn), lambda l, j: (l, 0, j)),
        out_shape=jax.ShapeDtypeStruct((depth, COND_ROWS, 6 * D_MODEL), F32),
        compiler_params=_params(2),
        name="modulation",
    )(cond, w_mod, b_mod.reshape(depth, 1, 6 * D_MODEL))


def _in_kernel(x_ref, sh_ref, sc_ref, ng_ref, w_ref, wa_ref, ba_ref,
               q_ref, k_ref, v_ref, g_ref, la_ref, cu_ref, dq_ref, dk_ref, dv_ref, gt_ref):
    x = x_ref[...]
    ms = jnp.mean(x * x, axis=-1, keepdims=True)
    h = x * lax.rsqrt(ms + EPS) * ng_ref[...]
    h = (h * (1.0 + sc_ref[...]) + sh_ref[...]).astype(BF16)

    def proj(c0, width, out_ref, fn=None):
        step = 512
        for a in range(0, width, step):
            b = min(a + step, width)
            r = _dot(h, w_ref[:, c0 + a:c0 + b])
            out_ref[:, a:b] = r if fn is None else fn(r)

    proj(C_Q, GLA_QW, q_ref)
    proj(C_K, GLA_QW, k_ref)
    proj(C_V, GLA_VW, v_ref)
    proj(C_G, GLA_VW, g_ref)
    proj(C_CU, 2 * CONV_CH, cu_ref)
    proj(C_DQ, DIFF_QW, dq_ref)
    proj(C_DK, DIFF_QW, dk_ref)
    proj(C_DV, DIFF_VW, dv_ref)
    proj(C_GT, N_BRANCH * D_MODEL, gt_ref, jax.nn.sigmoid)
    lr = _dot(h, w_ref[:, C_LR:C_LR + LR_PAD]).astype(BF16)
    xa = _dot(lr, wa_ref[...]) + ba_ref[...]
    la_ref[...] = (jnp.minimum(xa, 0.0) - jnp.log1p(jnp.exp(-jnp.abs(xa)))) * (1.0 / GLA_TEMP)


def _in_projection(x, mod4, norm_g, w_in_p, wa_p, ba_p, l, blk):
    n = x.shape[0]
    row = blk.cond_row

    def tok(width):
        return pl.BlockSpec((TOKEN_BLOCK, width), lambda t: (t, 0))

    widths = (GLA_QW, GLA_QW, GLA_VW, GLA_VW, 2 * GLA_QW, 2 * CONV_CH, DIFF_QW, DIFF_QW, DIFF_VW, N_BRANCH * D_MODEL)
    return pl.pallas_call(
        _in_kernel,
        grid=(blk.n_blocks,),
        in_specs=[
            tok(D_MODEL),
            pl.BlockSpec((None, None, 1, D_MODEL), lambda t: (l, row(t), 0, 0)),
            pl.BlockSpec((None, None, 1, D_MODEL), lambda t: (l, row(t), 0, 1)),
            _resident((None, 1, D_MODEL), lambda t: (l, 0, 0)),
            _resident((None, D_MODEL, IN_PACKED), lambda t: (l, 0, 0)),
            _resident((None, LR_PAD, 2 * GLA_QW), lambda t: (l, 0, 0)),
            _resident((None, 1, 2 * GLA_QW), lambda t: (l, 0, 0)),
        ],
        out_specs=[tok(w) for w in widths],
        out_shape=[jax.ShapeDtypeStruct((n, w), F32) for w in widths],
        compiler_params=_params(1),
        name=f"in_projection_{l}",
    )(x, mod4, mod4, norm_g, w_in_p, wa_p, ba_p)


def _gla_direction(q_ref, k_ref, v_ref, la_ref, s_scr, o_ref, tri, causal, lo, order, mid, last):
    la = la_ref[...]
    hi = la.astype(BF16)
    r1 = la - hi.astype(F32)
    md = r1.astype(BF16)
    lw = (r1 - md.astype(F32)).astype(BF16)
    b = _dot(tri, hi) + _dot(tri, md) + _dot(tri, lw)
    q = q_ref[...] * (GLA_DK ** -0.5)
    k = k_ref[...]
    c = GLA_CHUNK
    pair = 2 * GLA_DK
    for n in order:
        sl = slice(n * c, (n + 1) * c)
        bn = b[sl]
        bmid = bn[mid:mid + 1]
        blast = bn[last:last + 1]
        qn, kn = q[sl], k[sl]
        vn = v_ref[sl, :].astype(BF16)
        qe = qn * jnp.exp(bn)
        qt = qn * jnp.exp(bn - bmid)
        kt = (kn * jnp.exp(bmid - bn)).astype(BF16)
        ku = (kn * jnp.exp(blast - bn)).astype(BF16)
        decay = jnp.exp(jnp.broadcast_to(blast, (LANES, N_HEADS * GLA_DK)).T)
        s_old = s_scr[...]
        s_b16 = s_old.astype(BF16)
        for p in range(N_HEADS // 2):
            cs = slice(p * pair, (p + 1) * pair)
            rs = slice(p * pair, (p + 1) * pair)
            qt_p, qe_p = qt[:, cs], qe[:, cs]
            lhs = jnp.concatenate([jnp.where(lo, qt_p, 0.0), jnp.where(lo, 0.0, qt_p)], axis=0).astype(BF16)
            att = _dot_nt(lhs, kt[:, cs])
            att0 = jnp.where(causal, att[0:c], 0.0).astype(BF16)
            att1 = jnp.where(causal, att[c:2 * c], 0.0).astype(BF16)
            s_p = s_b16[rs]
            v0 = vn[:, (2 * p) * GLA_DV:(2 * p + 1) * GLA_DV]
            v1 = vn[:, (2 * p + 1) * GLA_DV:(2 * p + 2) * GLA_DV]
            o_ref[sl, (2 * p) * GLA_DV:(2 * p + 1) * GLA_DV] = (
                _dot(jnp.where(lo, qe_p, 0.0).astype(BF16), s_p) + _dot(att0, v0))
            o_ref[sl, (2 * p + 1) * GLA_DV:(2 * p + 2) * GLA_DV] = (
                _dot(jnp.where(lo, 0.0, qe_p).astype(BF16), s_p) + _dot(att1, v1))
            uf = _dot_tn(ku[:, cs], vn[:, (2 * p) * GLA_DV:(2 * p + 2) * GLA_DV])
            u_p = jnp.concatenate([uf[0:GLA_DK, 0:GLA_DV], uf[GLA_DK:2 * GLA_DK, GLA_DV:2 * GLA_DV]], axis=0)
            s_scr[rs, :] = s_old[rs] * decay[rs] + u_p


def _gla_kernel(qf_ref, kf_ref, vf_ref, laf_ref, qb_ref, kb_ref, vb_ref, lab_ref, s0_ref,
                of_ref, ob_ref, sout_ref, sf_scr, sb_scr, *, blk):
    t = pl.program_id(0)
    pos = blk.pos(t)

    @pl.when(pos == 0)
    def _():
        sf_scr[...] = s0_ref[0]
        sb_scr[...] = s0_ref[1]

    tb, c = TOKEN_BLOCK, GLA_CHUNK
    shift = int(math.log2(c))
    row = lax.broadcasted_iota(jnp.int32, (tb, tb), 0)
    col = lax.broadcasted_iota(jnp.int32, (tb, tb), 1)
    same = jnp.right_shift(row, shift) == jnp.right_shift(col, shift)
    tri_f = jnp.where(same, jnp.where(col <= row, 1.0, 0.0), 0.0).astype(BF16)
    tri_b = jnp.where(same, jnp.where(col >= row, 1.0, 0.0), 0.0).astype(BF16)
    r64 = lax.broadcasted_iota(jnp.int32, (c, c), 0)
    c64 = lax.broadcasted_iota(jnp.int32, (c, c), 1)
    lo = lax.broadcasted_iota(jnp.int32, (c, 2 * GLA_DK), 1) < GLA_DK
    n_chunks = tb // c
    _gla_direction(qf_ref, kf_ref, vf_ref, laf_ref, sf_scr, of_ref, tri_f, c64 <= r64, lo,
                   tuple(range(n_chunks)), c // 2 - 1, c - 1)
    _gla_direction(qb_ref, kb_ref, vb_ref, lab_ref, sb_scr, ob_ref, tri_b, c64 >= r64, lo,
                   tuple(reversed(range(n_chunks))), c // 2, 0)

    @pl.when(pos == blk.last_pos(t))
    def _():
        sout_ref[0] = sf_scr[...]
        sout_ref[1] = sb_scr[...]


def _gla(q, k, v, la, s0, blk):
    n = q.shape[0]
    rev = blk.rev
    hk, hv = N_HEADS * GLA_DK, N_HEADS * GLA_DV

    def fwd(width, cb=0):
        return pl.BlockSpec((TOKEN_BLOCK, width), lambda t: (t, cb))

    def bwd(width, cb=0):
        return pl.BlockSpec((TOKEN_BLOCK, width), lambda t: (rev(t), cb))

    state_spec = pl.BlockSpec((None, 2, hk, GLA_DV), lambda t: (blk.seq(t), 0, 0, 0))
    return pl.pallas_call(
        functools.partial(_gla_kernel, blk=blk),
        grid=(blk.n_blocks,),
        in_specs=[fwd(hk), fwd(hk), fwd(hv), fwd(hk, 0), bwd(hk), bwd(hk), bwd(hv), bwd(hk, 1), state_spec],
        out_specs=[fwd(hv), bwd(hv), state_spec],
        out_shape=[jax.ShapeDtypeStruct((n, hv), F32), jax.ShapeDtypeStruct((n, hv), F32),
                   jax.ShapeDtypeStruct((blk.n_seq, 2, hk, GLA_DV), F32)],
        scratch_shapes=[pltpu.VMEM((hk, GLA_DV), F32), pltpu.VMEM((hk, GLA_DV), F32)],
        compiler_params=_params(1),
        name="gla_scan",
    )(q, k, v, la, q, k, v, la, s0)


def _conv_kernel(prev_ref, cur_ref, next_ref, w_ref, b_ref, lg_ref, lb_ref, o_ref, y_scr, *, blk):
    t = pl.program_id(0)
    pos = blk.pos(t)
    ch = CONV_CH

    def glu(u):
        return u[:, :ch] * jax.nn.sigmoid(u[:, ch:])

    y_scr[0:HALO, :] = jnp.where(pos == 0, 0.0, glu(prev_ref[...]))
    y_scr[HALO:HALO + TOKEN_BLOCK, :] = glu(cur_ref[...])
    y_scr[HALO + TOKEN_BLOCK:, :] = jnp.where(pos == blk.last_pos(t), 0.0, glu(next_ref[...]))

    rows = 32
    for r0 in range(0, TOKEN_BLOCK, rows):
        acc = jnp.zeros((rows, ch), F32)
        for w in range(CONV_WIDTH):
            a = r0 + w + HALO - CONV_PAD
            acc = acc + y_scr[a:a + rows, :] * w_ref[w:w + 1, :]
        y = acc + b_ref[...]
        mu = jnp.mean(y, axis=-1, keepdims=True)
        d = y - mu
        var = jnp.mean(d * d, axis=-1, keepdims=True)
        z = d * lax.rsqrt(var + EPS) * lg_ref[...] + lb_ref[...]
        o_ref[r0:r0 + rows, :] = _silu(z)


def _conv(cu, conv_w, conv_b, ln_g, ln_b, l, blk):
    n = cu.shape[0]
    hb = TOKEN_BLOCK // HALO
    n_halo = n // HALO
    vec = lambda: _resident((None, 1, CONV_CH), lambda t: (l, 0, 0))
    return pl.pallas_call(
        functools.partial(_conv_kernel, blk=blk),
        grid=(blk.n_blocks,),
        in_specs=[
            pl.BlockSpec((HALO, 2 * CONV_CH), lambda t: (jnp.maximum(t * hb - 1, 0), 0)),
            pl.BlockSpec((TOKEN_BLOCK, 2 * CONV_CH), lambda t: (t, 0)),
            pl.BlockSpec((HALO, 2 * CONV_CH), lambda t: (jnp.minimum((t + 1) * hb, n_halo - 1), 0)),
            _resident((None, CONV_WIDTH, CONV_CH), lambda t: (l, 0, 0)),
            vec(), vec(), vec(),
        ],
        out_specs=pl.BlockSpec((TOKEN_BLOCK, CONV_CH), lambda t: (t, 0)),
        out_shape=jax.ShapeDtypeStruct((n, CONV_CH), F32),
        scratch_shapes=[pltpu.VMEM((TOKEN_BLOCK + 2 * HALO, CONV_CH), F32)],
        compiler_params=_params(1),
        name=f"conv_module_{l}",
    )(cu, cu, cu, conv_w, conv_b, ln_g, ln_b)


def _rope(x, cos, sin_signed, first_half):
    partner = jnp.where(first_half, pltpu.roll(x, LANES - ROPE_FREQS, 1), pltpu.roll(x, ROPE_FREQS, 1))
    return x * cos + partner * sin_signed


def _diff_kernel(*refs, rope, lam_init, n_self, n_cache, tq):
    if rope:
        (q_ref, k_ref, v_ref, ck_ref, cv_ref, cosq_ref, sinq_ref, cosk_ref, sink_ref, lam_ref, g_ref,
         _prev_ref, o_ref, k_scr, v_scr) = refs
    else:
        q_ref, k_ref, v_ref, lam_ref, g_ref, o_ref, k_scr, v_scr = refs
    slab = 2 * DIFF_DH
    lane = lax.broadcasted_iota(jnp.int32, (1, slab), 1)
    first_half = (lane % (2 * ROPE_FREQS)) < ROPE_FREQS
    lo = lane < DIFF_DH

    @pl.when(pl.program_id(1) == 0)
    def _():
        rows = 256
        for r0 in range(0, n_self, rows):
            rs = slice(r0, r0 + rows)
            for h in range(N_HEADS):
                cs = slice(h * slab, (h + 1) * slab)
                kx = k_ref[rs, cs]
                if rope:
                    kx = _rope(kx, cosk_ref[rs, :], sink_ref[rs, :], first_half)
                k_scr[rs, cs] = kx.astype(BF16)
            v_scr[rs, :] = v_ref[rs, :].astype(BF16)
        if n_cache:
            k_scr[n_self:n_self + n_cache, :] = ck_ref[...].astype(BF16)
            v_scr[n_self:n_self + n_cache, :] = cv_ref[...].astype(BF16)

    lv = lam_ref[...]
    lam = (jnp.exp(jnp.sum(lv[0:1] * lv[1:2], axis=-1, keepdims=True))
           - jnp.exp(jnp.sum(lv[2:3] * lv[3:4], axis=-1, keepdims=True)) + lam_init)
    scale = DIFF_DH ** -0.5
    for h in range(N_HEADS):
        cs = slice(h * slab, (h + 1) * slab)
        qh = q_ref[:, cs]
        if rope:
            qh = _rope(qh, cosq_ref[...], sinq_ref[...], first_half)
        qh = qh * scale
        kh = k_scr[:, cs]
        s0 = _dot_nt(jnp.where(lo, qh, 0.0).astype(BF16), kh)
        s1 = _dot_nt(jnp.where(lo, 0.0, qh).astype(BF16), kh)
        p0 = jnp.exp(s0 - jnp.max(s0, axis=-1, keepdims=True))
        p1 = jnp.exp(s1 - jnp.max(s1, axis=-1, keepdims=True))
        r0 = 1.0 / jnp.sum(p0, axis=-1, keepdims=True)
        r1 = lam / jnp.sum(p1, axis=-1, keepdims=True)
        w = (p0 * r0 - p1 * r1).astype(BF16)
        o = _dot(w, v_scr[:, h * DIFF_DV:(h + 1) * DIFF_DV])
        ms = jnp.mean(o * o, axis=-1, keepdims=True)
        o_ref[:, h * DIFF_DV:(h + 1) * DIFF_DV] = o * lax.rsqrt(ms + EPS) * g_ref[...] * (1.0 - lam_init)


def _diff_attention(dq, dk, dv, cache_k, cache_v, rope_tabs, diff_lambda, subln_g, l, blk, out_like):
    n = dq.shape[0]
    lam_init = 0.8 - 0.6 * math.exp(-0.3 * l)
    n_p = blk.n_pb * TOKEN_BLOCK
    ls = blk.bps * TOKEN_BLOCK
    lam_spec = lambda: pl.BlockSpec((None, 4, DIFF_DH), lambda s, i: (l, 0, 0))
    g_spec = lambda: pl.BlockSpec((None, 1, DIFF_DV), lambda s, i: (l, 0, 0))

    o_prompt = pl.pallas_call(
        functools.partial(_diff_kernel, rope=False, lam_init=lam_init, n_self=TOKEN_BLOCK, n_cache=0, tq=TOKEN_BLOCK),
        grid=(blk.n_pb, 1),
        in_specs=[pl.BlockSpec((TOKEN_BLOCK, DIFF_QW), lambda s, i: (s, 0)),
                  pl.BlockSpec((TOKEN_BLOCK, DIFF_QW), lambda s, i: (s, 0)),
                  pl.BlockSpec((TOKEN_BLOCK, DIFF_VW), lambda s, i: (s, 0)),
                  lam_spec(), g_spec()],
        out_specs=pl.BlockSpec((TOKEN_BLOCK, DIFF_VW), lambda s, i: (s, 0)),
        out_shape=jax.ShapeDtypeStruct((n, DIFF_VW), F32),
        scratch_shapes=[pltpu.VMEM((TOKEN_BLOCK, DIFF_QW), BF16), pltpu.VMEM((TOKEN_BLOCK, DIFF_VW), BF16)],
        compiler_params=_params(2),
        name=f"diff_attn_prompt_{l}",
    )(dq, dk, dv, diff_lambda, subln_g)

    tq = TOKEN_BLOCK
    n_cache = cache_k.shape[2]
    cos_t, sin_t = rope_tabs
    seq0 = n_p // ls
    assert seq0 * ls == n_p
    q0 = n_p // tq
    o_all = pl.pallas_call(
        functools.partial(_diff_kernel, rope=True, lam_init=lam_init, n_self=ls, n_cache=n_cache, tq=tq),
        grid=(blk.n_ss, ls // tq),
        in_specs=[pl.BlockSpec((tq, DIFF_QW), lambda s, i: (q0 + s * (ls // tq) + i, 0)),
                  pl.BlockSpec((ls, DIFF_QW), lambda s, i: (seq0 + s, 0)),
                  pl.BlockSpec((ls, DIFF_VW), lambda s, i: (seq0 + s, 0)),
                  pl.BlockSpec((None, None, n_cache, DIFF_QW), lambda s, i: (s, l, 0, 0)),
                  pl.BlockSpec((None, None, n_cache, DIFF_VW), lambda s, i: (s, l, 0, 0)),
                  pl.BlockSpec((tq, LANES), lambda s, i: (i, 0)),
                  pl.BlockSpec((tq, LANES), lambda s, i: (i, 0)),
                  _resident((ls, LANES), lambda s, i: (0, 0)),
                  _resident((ls, LANES), lambda s, i: (0, 0)),
                  lam_spec(), g_spec(),
                  pl.BlockSpec(memory_space=pl.ANY)],
        out_specs=pl.BlockSpec((tq, DIFF_VW), lambda s, i: (q0 + s * (ls // tq) + i, 0)),
        out_shape=jax.ShapeDtypeStruct((n, DIFF_VW), F32),
        scratch_shapes=[pltpu.VMEM((ls + n_cache, DIFF_QW), BF16), pltpu.VMEM((ls + n_cache, DIFF_VW), BF16)],
        input_output_aliases={11: 0},
        compiler_params=_params(2),
        name=f"diff_attn_latent_{l}",
    )(dq, dk, dv, cache_k, cache_v, cos_t, sin_t, cos_t, sin_t, diff_lambda, subln_g, o_prompt)
    return o_all


def _post_kernel(x_ref, of_ref, ob_ref, g_ref, oc_ref, od_ref, gt_ref, g1_ref, sh2_ref, sc2_ref, g2_ref,
                 gng_ref, n2g_ref, fng_ref, wb_ref, wo_ref, wfi_ref, wfo_ref, out_ref, acc_ref, *, final):
    o = of_ref[...] + ob_ref[...]
    parts = []
    for h in range(N_HEADS):
        cs = slice(h * GLA_DV, (h + 1) * GLA_DV)
        oh = o[:, cs]
        ms = jnp.mean(oh * oh, axis=-1, keepdims=True)
        parts.append((oh * lax.rsqrt(ms + EPS) * gng_ref[...] * _silu(g_ref[:, cs])).astype(BF16))
    o_gla = jnp.concatenate(parts, axis=1)
    o_conv = oc_ref[...].astype(BF16)
    o_diff = od_ref[...].astype(BF16)
    b1 = GLA_VW
    b2 = b1 + CONV_CH
    b3 = b2 + DIFF_VW
    half = D_MODEL // 2
    merged = []
    for c0 in range(0, D_MODEL, half):
        cs = slice(c0, c0 + half)
        m = gt_ref[:, c0:c0 + half] * _dot(o_gla, wb_ref[0:b1, cs])
        m = m + gt_ref[:, D_MODEL + c0:D_MODEL + c0 + half] * _dot(o_conv, wb_ref[b1:b2, cs])
        m = m + gt_ref[:, 2 * D_MODEL + c0:2 * D_MODEL + c0 + half] * _dot(o_diff, wb_ref[b2:b3, cs])
        merged.append(m.astype(BF16))
    merged = jnp.concatenate(merged, axis=1)
    x1 = x_ref[...] + g1_ref[...] * _dot(merged, wo_ref[...])

    ms = jnp.mean(x1 * x1, axis=-1, keepdims=True)
    h2 = x1 * lax.rsqrt(ms + EPS) * n2g_ref[...]
    h2 = (h2 * (1.0 + sc2_ref[...]) + sh2_ref[...]).astype(BF16)
    for i, c0 in enumerate(range(0, FFN_HIDDEN, FFN_CHUNK)):
        a = _dot(h2, wfi_ref[:, c0:c0 + FFN_CHUNK])
        b = _dot(h2, wfi_ref[:, FFN_HIDDEN + c0:FFN_HIDDEN + c0 + FFN_CHUNK])
        u = (_silu(a) * b).astype(BF16)
        y = _dot(u, wfo_ref[c0:c0 + FFN_CHUNK, :])
        if i == 0:
            acc_ref[...] = y
        else:
            acc_ref[...] += y
    x2 = x1 + g2_ref[...] * acc_ref[...]
    if final:
        ms = jnp.mean(x2 * x2, axis=-1, keepdims=True)
        out_ref[...] = x2 * lax.rsqrt(ms + EPS) * fng_ref[...]
    else:
        out_ref[...] = x2


def _post(x, o_f, o_b, g, o_conv, o_diff, gates, mod4, gla_norm_g, norm2_g, final_g, wb, wo, wfi, wfo, l, blk, final):
    n = x.shape[0]
    row = blk.cond_row

    def tok(width):
        return pl.BlockSpec((TOKEN_BLOCK, width), lambda t: (t, 0))

    def mod(j):
        return pl.BlockSpec((None, None, 1, D_MODEL), lambda t: (l, row(t), 0, j))

    return pl.pallas_call(
        functools.partial(_post_kernel, final=final),
        grid=(blk.n_blocks,),
        in_specs=[
            tok(D_MODEL), tok(GLA_VW), tok(GLA_VW), tok(GLA_VW), tok(CONV_CH), tok(DIFF_VW), tok(N_BRANCH * D_MODEL),
            mod(2), mod(3), mod(4), mod(5),
            _resident((None, 1, GLA_DV), lambda t: (l, 0, 0)),
            _resident((None, 1, D_MODEL), lambda t: (l, 0, 0)),
            _resident((1, D_MODEL), lambda t: (0, 0)),
            _resident((None, GLA_VW + CONV_CH + DIFF_VW, D_MODEL), lambda t: (l, 0, 0)),
            _resident((None, D_MODEL, D_MODEL), lambda t: (l, 0, 0)),
            _resident((None, D_MODEL, 2 * FFN_HIDDEN), lambda t: (l, 0, 0)),
            _resident((None, FFN_HIDDEN, D_MODEL), lambda t: (l, 0, 0)),
        ],
        out_specs=tok(D_MODEL),
        out_shape=jax.ShapeDtypeStruct((n, D_MODEL), F32),
        scratch_shapes=[pltpu.VMEM((TOKEN_BLOCK, D_MODEL), F32)],
        compiler_params=_params(1),
        name=f"merge_ffn_{l}",
    )(x, o_f, o_b, g, o_conv, o_diff, gates, mod4, mod4, mod4, mod4, gla_norm_g, norm2_g, final_g, wb, wo, wfi, wfo)


def _rope_tables(n_tokens):
    rows = n_tokens // GRID_W
    row = jnp.repeat(jnp.arange(rows, dtype=F32), GRID_W)
    col = jnp.tile(jnp.arange(GRID_W, dtype=F32), rows)
    inv = ROPE_BASE ** (-jnp.arange(ROPE_FREQS, dtype=F32) / ROPE_FREQS)
    ang = jnp.stack([row[:, None] * inv, col[:, None] * inv], axis=1)
    cos, sin = jnp.cos(ang), jnp.sin(ang)
    cos_g = jnp.concatenate([cos, cos], axis=-1).reshape(n_tokens, 4 * ROPE_FREQS)
    sin_g = jnp.concatenate([-sin, sin], axis=-1).reshape(n_tokens, 4 * ROPE_FREQS)
    return jnp.tile(cos_g, (1, 2)), jnp.tile(sin_g, (1, 2))


def kernel(x_prompt, x_sample, cache_diff_k, cache_diff_v, state_gla, c, c_ctx, w_mod, b_mod, norm1_g, norm2_g,
           w_in, gla_w_a2, gla_b_a, gla_norm_g, conv_w, conv_b, conv_ln_g, conv_ln_b, diff_lambda, diff_subln_g,
           w_branch, w_out, w_ffn_in, w_ffn_out, final_norm_g):
    batch, seq, d = x_prompt.shape
    dec_batch, dec_seq, _ = x_sample.shape
    depth = w_mod.shape[0]
    assert d == D_MODEL and seq == TOKEN_BLOCK and dec_seq % TOKEN_BLOCK == 0 and dec_batch + 1 <= COND_ROWS
    blk = _Blocks(batch, dec_batch, dec_seq // TOKEN_BLOCK)
    n_p = batch * seq

    x = jnp.concatenate([x_prompt.reshape(n_p, d), x_sample.reshape(dec_batch * dec_seq, d)], axis=0)
    cond = jnp.concatenate([c_ctx[None, :], c, jnp.zeros((COND_ROWS - 1 - dec_batch, d), F32)], axis=0)
    w_in_p = jnp.concatenate(
        [w_in[..., OFFS[0]:OFFS[4]], w_in[..., OFFS[5]:OFFS[10]], w_in[..., OFFS[4]:OFFS[5]],
         jnp.zeros((depth, d, LR_PAD - 2 * GLA_RANK), F32)], axis=-1).astype(BF16)
    wa_p = jnp.zeros((depth, LR_PAD, 2 * GLA_QW), F32)
    wa_p = wa_p.at[:, 0:GLA_RANK, 0:GLA_QW].set(gla_w_a2[:, 0])
    wa_p = wa_p.at[:, GLA_RANK:2 * GLA_RANK, GLA_QW:].set(gla_w_a2[:, 1]).astype(BF16)
    ba_p = gla_b_a.reshape(depth, 1, 2 * GLA_QW)
    wb = w_branch.astype(BF16)
    wo = w_out.astype(BF16)
    wfi = w_ffn_in.astype(BF16)
    wfo = w_ffn_out.astype(BF16)
    vec = lambda a: a.reshape(depth, 1, a.shape[-1])
    rope_tabs = _rope_tables(dec_seq)
    ck = cache_diff_k.reshape(dec_batch, depth, -1, DIFF_QW)
    cv = cache_diff_v.reshape(dec_batch, depth, -1, DIFF_VW)
    hk = N_HEADS * GLA_DK
    s0_all = jnp.concatenate(
        [jnp.zeros((batch, depth, 2, hk, GLA_DV), F32), state_gla.reshape(dec_batch, depth, 2, hk, GLA_DV)], axis=0)

    mod4 = _modulation(cond, w_mod, b_mod).reshape(depth, COND_ROWS, 1, 6 * d)

    new_k, new_v, new_s = [], [], []
    for l in range(depth):
        q, k, v, g, la, cu, dq, dk, dv, gates = _in_projection(x, mod4, vec(norm1_g), w_in_p, wa_p, ba_p, l, blk)
        o_f, o_b, s_out = _gla(q, k, v, la, s0_all[:, l], blk)
        o_conv = _conv(cu, conv_w, vec(conv_b), vec(conv_ln_g), vec(conv_ln_b), l, blk)
        o_diff = _diff_attention(dq, dk, dv, ck, cv, rope_tabs, diff_lambda, vec(diff_subln_g), l, blk, x)
        x = _post(x, o_f, o_b, g, o_conv, o_diff, gates, mod4, vec(gla_norm_g), vec(norm2_g),
                  final_norm_g.reshape(1, d), wb, wo, wfi, wfo, l, blk, final=(l == depth - 1))
        new_k.append(dk[:n_p].reshape(batch, seq, N_HEADS, 2, DIFF_DH))
        new_v.append(dv[:n_p].reshape(batch, seq, N_HEADS, DIFF_DV))
        new_s.append(s_out[:batch].reshape(batch, 2, N_HEADS, GLA_DK, GLA_DV))
    y_prompt = x[:n_p].reshape(batch, seq, d)
    y_sample = x[n_p:].reshape(dec_batch, dec_seq, d)
    return (y_prompt, y_sample, jnp.stack(new_k, axis=1), jnp.stack(new_v, axis=1), jnp.stack(new_s, axis=1))
```

```python
import functools
import math

import jax
import jax.numpy as jnp
from jax import lax
from jax.experimental import pallas as pl
from jax.experimental.pallas import tpu as pltpu

F32 = jnp.float32
BF16 = jnp.bfloat16

D_MODEL = 1024
N_HEADS = 4
GLA_DK = 64
GLA_DV = 128
GLA_RANK = 16
GLA_TEMP = 16.0
GLA_CHUNK = 64
CONV_CH = 512
CONV_WIDTH = 31
CONV_PAD = (CONV_WIDTH - 1) // 2
DIFF_DH = 64
DIFF_DV = 128
ROPE_FREQS = DIFF_DH // 4
ROPE_BASE = 10000.0
GRID_W = 64
FFN_HIDDEN = -(-8 * D_MODEL // (3 * 256)) * 256
EPS = 1e-6
N_BRANCH = 3

GLA_QW = N_HEADS * GLA_DK
GLA_VW = N_HEADS * GLA_DV
DIFF_QW = N_HEADS * 2 * DIFF_DH
DIFF_VW = N_HEADS * DIFF_DV
SPLITS = (GLA_QW, GLA_QW, GLA_VW, GLA_VW, 2 * GLA_RANK, 2 * CONV_CH, DIFF_QW, DIFF_QW, DIFF_VW, N_BRANCH * D_MODEL)
OFFS = tuple(sum(SPLITS[:i]) for i in range(len(SPLITS) + 1))

TOKEN_BLOCK = 256
LANES = 128
SUBLANES = 8
HALO = 16
CONV_ROWS = TOKEN_BLOCK + 2 * HALO - SUBLANES
COND_ROWS = 8
LR_PAD = LANES
FFN_CHUNK = 256
VMEM_LIMIT = 56 * 1024 * 1024

C_Q, C_K, C_V, C_G = 0, GLA_QW, 2 * GLA_QW, 2 * GLA_QW + GLA_VW
C_CU = C_G + GLA_VW
C_DQ = C_CU + 2 * CONV_CH
C_DK = C_DQ + DIFF_QW
C_DV = C_DK + DIFF_QW
C_GT = C_DV + DIFF_VW
C_LR = C_GT + N_BRANCH * D_MODEL
IN_PACKED = C_LR + LR_PAD


def _params(n_axes=1):
    return pltpu.CompilerParams(dimension_semantics=("arbitrary",) * n_axes, vmem_limit_bytes=VMEM_LIMIT)


def _resident(shape, index_map):
    return pl.BlockSpec(shape, index_map, pipeline_mode=pl.Buffered(1))


def _silu(x):
    return x * jax.nn.sigmoid(x)


def _dot(a, b):
    return jnp.dot(a, b, preferred_element_type=F32)


def _dot_nt(a, b):
    return lax.dot_general(a, b, (((1,), (1,)), ((), ())), preferred_element_type=F32)


def _dot_tn(a, b):
    return lax.dot_general(a, b, (((0,), (0,)), ((), ())), preferred_element_type=F32)


class _Blocks:
    def __init__(self, n_pb, n_ss, bps):
        self.n_pb, self.n_ss, self.bps = n_pb, n_ss, bps
        self.n_blocks = n_pb + n_ss * bps
        self.n_seq = n_pb + n_ss

    def is_prompt(self, t):
        return t < self.n_pb

    def cond_row(self, t):
        return jnp.where(t < self.n_pb, 0, 1 + (t - self.n_pb) // self.bps)

    def seq(self, t):
        return jnp.where(t < self.n_pb, t, self.n_pb + (t - self.n_pb) // self.bps)

    def pos(self, t):
        return jnp.where(t < self.n_pb, 0, (t - self.n_pb) % self.bps)

    def last_pos(self, t):
        return jnp.where(t < self.n_pb, 0, self.bps - 1)

    def rev(self, t):
        u = t - self.n_pb
        return jnp.where(t < self.n_pb, t, self.n_pb + (u // self.bps) * self.bps + (self.bps - 1 - u % self.bps))


def _mod_kernel(c_ref, w_ref, b_ref, o_ref):
    s = _silu(c_ref[...]).astype(BF16)
    o_ref[...] = _dot(s, w_ref[...].astype(BF16)) + b_ref[...]


def _modulation(cond, w_mod, b_mod):
    depth = w_mod.shape[0]
    tn = 1536
    return pl.pallas_call(
        _mod_kernel,
        grid=(depth, 6 * D_MODEL // tn),
        in_specs=[
            pl.BlockSpec((COND_ROWS, D_MODEL), lambda l, j: (0, 0)),
            pl.BlockSpec((None, D_MODEL, tn), lambda l, j: (l, 0, j)),
            pl.BlockSpec((None, 1, tn), lambda l, j: (l, 0, j)),
        ],
        out_specs=pl.BlockSpec((None, COND_ROWS, tn), lambda l, j: (l, 0, j)),
        out_shape=jax.ShapeDtypeStruct((depth, COND_ROWS, 6 * D_MODEL), F32),
        compiler_params=_params(2),
        name="modulation",
    )(cond, w_mod, b_mod.reshape(depth, 1, 6 * D_MODEL))


def _pack_kernel(w_ref, o_ref):
    w = w_ref[...]
    o_ref[:, 0:C_CU] = w[:, OFFS[0]:OFFS[4]].astype(BF16)
    o_ref[:, C_CU:C_LR] = w[:, OFFS[5]:OFFS[10]].astype(BF16)
    lane = lax.broadcasted_iota(jnp.int32, (1, LR_PAD), 1)
    o_ref[:, C_LR:IN_PACKED] = jnp.where(lane < 2 * GLA_RANK, w[:, OFFS[4]:OFFS[4] + LR_PAD], 0.0).astype(BF16)


def _pack_in_weight(w_in):
    depth, d, width = w_in.shape
    rows = 256
    return pl.pallas_call(
        _pack_kernel,
        grid=(depth, d // rows),
        in_specs=[pl.BlockSpec((None, rows, width), lambda l, i: (l, i, 0))],
        out_specs=pl.BlockSpec((None, rows, IN_PACKED), lambda l, i: (l, i, 0)),
        out_shape=jax.ShapeDtypeStruct((depth, d, IN_PACKED), BF16),
        compiler_params=_params(2),
        name="pack_in_weight",
    )(w_in)


def _in_kernel(x_ref, sh_ref, sc_ref, ng_ref, w_ref, wa_ref, ba_ref,
               q_ref, k_ref, v_ref, g_ref, la_ref, cu_ref, dq_ref, dk_ref, dv_ref, gt_ref):
    x = x_ref[...]
    ms = jnp.mean(x * x, axis=-1, keepdims=True)
    h = x * lax.rsqrt(ms + EPS) * ng_ref[...]
    h = (h * (1.0 + sc_ref[...]) + sh_ref[...]).astype(BF16)

    def proj(c0, width, out_ref, fn=None):
        step = 512
        for a in range(0, width, step):
            b = min(a + step, width)
            r = _dot(h, w_ref[:, c0 + a:c0 + b])
            out_ref[:, a:b] = r if fn is None else fn(r)

    proj(C_Q, GLA_QW, q_ref)
    proj(C_K, GLA_QW, k_ref)
    proj(C_V, GLA_VW, v_ref)
    proj(C_G, GLA_VW, g_ref)
    proj(C_CU, 2 * CONV_CH, cu_ref)
    proj(C_DQ, DIFF_QW, dq_ref)
    proj(C_DK, DIFF_QW, dk_ref)
    proj(C_DV, DIFF_VW, dv_ref)
    proj(C_GT, N_BRANCH * D_MODEL, gt_ref, jax.nn.sigmoid)
    lr = _dot(h, w_ref[:, C_LR:C_LR + LR_PAD]).astype(BF16)
    xa = _dot(lr, wa_ref[...]) + ba_ref[...]
    la_ref[...] = (jnp.minimum(xa, 0.0) - jnp.log1p(jnp.exp(-jnp.abs(xa)))) * (1.0 / GLA_TEMP)


def _in_projection(x, mod4, norm_g, w_in_p, wa_p, ba_p, l, blk):
    n = x.shape[0]
    row = blk.cond_row

    def tok(width):
        return pl.BlockSpec((TOKEN_BLOCK, width), lambda t: (t, 0))

    widths = (GLA_QW, GLA_QW, GLA_VW, GLA_VW, 2 * GLA_QW, 2 * CONV_CH, DIFF_QW, DIFF_QW, DIFF_VW, N_BRANCH * D_MODEL)
    return pl.pallas_call(
        _in_kernel,
        grid=(blk.n_blocks,),
        in_specs=[
            tok(D_MODEL),
            pl.BlockSpec((None, None, 1, D_MODEL), lambda t: (l, row(t), 0, 0)),
            pl.BlockSpec((None, None, 1, D_MODEL), lambda t: (l, row(t), 0, 1)),
            _resident((None, 1, D_MODEL), lambda t: (l, 0, 0)),
            _resident((None, D_MODEL, IN_PACKED), lambda t: (l, 0, 0)),
            _resident((None, LR_PAD, 2 * GLA_QW), lambda t: (l, 0, 0)),
            _resident((None, 1, 2 * GLA_QW), lambda t: (l, 0, 0)),
        ],
        out_specs=[tok(w) for w in widths],
        out_shape=[jax.ShapeDtypeStruct((n, w), F32) for w in widths],
        compiler_params=_params(1),
        name=f"in_projection_{l}",
    )(x, mod4, mod4, norm_g, w_in_p, wa_p, ba_p)


def _gla_prepare(q_ref, k_ref, v_ref, la_ref, tri, causal, lo, order, mid, last):
    la = la_ref[...]
    hi = la.astype(BF16)
    r1 = la - hi.astype(F32)
    md = r1.astype(BF16)
    lw = (r1 - md.astype(F32)).astype(BF16)
    b = _dot(tri, hi) + _dot(tri, md) + _dot(tri, lw)
    q = q_ref[...] * (GLA_DK ** -0.5)
    k = k_ref[...]
    c = GLA_CHUNK
    pair = 2 * GLA_DK
    chunks = []
    for n in order:
        sl = slice(n * c, (n + 1) * c)
        bn = b[sl]
        bmid = bn[mid:mid + 1]
        blast = bn[last:last + 1]
        qn, kn = q[sl], k[sl]
        vn = v_ref[sl, :].astype(BF16)
        qe = qn * jnp.exp(bn)
        qt = qn * jnp.exp(bn - bmid)
        kt = (kn * jnp.exp(bmid - bn)).astype(BF16)
        ku = (kn * jnp.exp(blast - bn)).astype(BF16)
        decay = jnp.exp(jnp.broadcast_to(blast, (LANES, N_HEADS * GLA_DK)).T)
        qes, atts, incs = [], [], []
        for p in range(N_HEADS // 2):
            cs = slice(p * pair, (p + 1) * pair)
            qt_p, qe_p = qt[:, cs], qe[:, cs]
            lhs = jnp.concatenate([jnp.where(lo, qt_p, 0.0), jnp.where(lo, 0.0, qt_p)], axis=0).astype(BF16)
            att = _dot_nt(lhs, kt[:, cs])
            atts += [jnp.where(causal, att[0:c], 0.0).astype(BF16), jnp.where(causal, att[c:2 * c], 0.0).astype(BF16)]
            qes += [jnp.where(lo, qe_p, 0.0).astype(BF16), jnp.where(lo, 0.0, qe_p).astype(BF16)]
            uf = _dot_tn(ku[:, cs], vn[:, (2 * p) * GLA_DV:(2 * p + 2) * GLA_DV])
            incs += [uf[0:GLA_DK, 0:GLA_DV], uf[GLA_DK:2 * GLA_DK, GLA_DV:2 * GLA_DV]]
        chunks.append((sl, qes, atts, vn, decay, jnp.concatenate(incs, axis=0)))
    return chunks


def _gla_scan(chunks, s_scr, o_ref):
    pair = 2 * GLA_DK
    s = s_scr[...]
    for sl, qes, atts, vn, decay, inc in chunks:
        s_b16 = s.astype(BF16)
        for h in range(N_HEADS):
            s_p = s_b16[(h // 2) * pair:(h // 2 + 1) * pair]
            cs = slice(h * GLA_DV, (h + 1) * GLA_DV)
            o_ref[sl, cs] = _dot(qes[h], s_p) + _dot(atts[h], vn[:, cs])
        s = s * decay + inc
    s_scr[...] = s


def _gla_kernel(qf_ref, kf_ref, vf_ref, laf_ref, qb_ref, kb_ref, vb_ref, lab_ref, s0_ref,
                of_ref, ob_ref, sout_ref, sf_scr, sb_scr, *, blk):
    t = pl.program_id(0)
    pos = blk.pos(t)

    @pl.when(pos == 0)
    def _():
        sf_scr[...] = s0_ref[0]
        sb_scr[...] = s0_ref[1]

    tb, c = TOKEN_BLOCK, GLA_CHUNK
    shift = int(math.log2(c))
    row = lax.broadcasted_iota(jnp.int32, (tb, tb), 0)
    col = lax.broadcasted_iota(jnp.int32, (tb, tb), 1)
    same = jnp.right_shift(row, shift) == jnp.right_shift(col, shift)
    tri_f = jnp.where(same, jnp.where(col <= row, 1.0, 0.0), 0.0).astype(BF16)
    tri_b = jnp.where(same, jnp.where(col >= row, 1.0, 0.0), 0.0).astype(BF16)
    r64 = lax.broadcasted_iota(jnp.int32, (c, c), 0)
    c64 = lax.broadcasted_iota(jnp.int32, (c, c), 1)
    lo = lax.broadcasted_iota(jnp.int32, (c, 2 * GLA_DK), 1) < GLA_DK
    n_chunks = tb // c
    fwd = _gla_prepare(qf_ref, kf_ref, vf_ref, laf_ref, tri_f, c64 <= r64, lo,
                       tuple(range(n_chunks)), c // 2 - 1, c - 1)
    bwd = _gla_prepare(qb_ref, kb_ref, vb_ref, lab_ref, tri_b, c64 >= r64, lo,
                       tuple(reversed(range(n_chunks))), c // 2, 0)
    _gla_scan(fwd, sf_scr, of_ref)
    _gla_scan(bwd, sb_scr, ob_ref)

    @pl.when(pos == blk.last_pos(t))
    def _():
        sout_ref[0] = sf_scr[...]
        sout_ref[1] = sb_scr[...]


def _gla(q, k, v, la, s0, blk):
    n = q.shape[0]
    rev = blk.rev
    hk, hv = N_HEADS * GLA_DK, N_HEADS * GLA_DV

    def fwd(width, cb=0):
        return pl.BlockSpec((TOKEN_BLOCK, width), lambda t: (t, cb))

    def bwd(width, cb=0):
        return pl.BlockSpec((TOKEN_BLOCK, width), lambda t: (rev(t), cb))

    state_spec = pl.BlockSpec((None, 2, hk, GLA_DV), lambda t: (blk.seq(t), 0, 0, 0))
    return pl.pallas_call(
        functools.partial(_gla_kernel, blk=blk),
        grid=(blk.n_blocks,),
        in_specs=[fwd(hk), fwd(hk), fwd(hv), fwd(hk, 0), bwd(hk), bwd(hk), bwd(hv), bwd(hk, 1), state_spec],
        out_specs=[fwd(hv), bwd(hv), state_spec],
        out_shape=[jax.ShapeDtypeStruct((n, hv), F32), jax.ShapeDtypeStruct((n, hv), F32),
                   jax.ShapeDtypeStruct((blk.n_seq, 2, hk, GLA_DV), F32)],
        scratch_shapes=[pltpu.VMEM((hk, GLA_DV), F32), pltpu.VMEM((hk, GLA_DV), F32)],
        compiler_params=_params(1),
        name="gla_scan",
    )(q, k, v, la, q, k, v, la, s0)


def _conv_kernel(prev_ref, cur_ref, next_ref, w_ref, b_ref, lg_ref, lb_ref, o_ref, y_scr, ys_scr, *, blk):
    t = pl.program_id(0)
    pos = blk.pos(t)
    ch = CONV_CH

    def glu(u):
        return u[:, :ch] * jax.nn.sigmoid(u[:, ch:])

    y_scr[0:HALO, :] = jnp.where(pos == 0, 0.0, glu(prev_ref[...]))
    y_scr[HALO:HALO + TOKEN_BLOCK, :] = glu(cur_ref[...])
    y_scr[HALO + TOKEN_BLOCK:, :] = jnp.where(pos == blk.last_pos(t), 0.0, glu(next_ref[...]))

    for s in range(1, SUBLANES):
        ys_scr[s - 1] = y_scr[s:s + CONV_ROWS, :]

    rows = 32
    first = HALO - CONV_PAD
    for r0 in range(0, TOKEN_BLOCK, rows):
        acc = jnp.zeros((rows, ch), F32)
        for w in range(CONV_WIDTH):
            s, a = (first + w) % SUBLANES, r0 + (first + w) // SUBLANES * SUBLANES
            tap = y_scr[a:a + rows, :] if s == 0 else ys_scr[s - 1, a:a + rows, :]
            acc = acc + tap * w_ref[w:w + 1, :]
        y = acc + b_ref[...]
        mu = jnp.mean(y, axis=-1, keepdims=True)
        d = y - mu
        var = jnp.mean(d * d, axis=-1, keepdims=True)
        z = d * lax.rsqrt(var + EPS) * lg_ref[...] + lb_ref[...]
        o_ref[r0:r0 + rows, :] = _silu(z)


def _conv(cu, conv_w, conv_b, ln_g, ln_b, l, blk):
    n = cu.shape[0]
    hb = TOKEN_BLOCK // HALO
    n_halo = n // HALO
    vec = lambda: _resident((None, 1, CONV_CH), lambda t: (l, 0, 0))
    return pl.pallas_call(
        functools.partial(_conv_kernel, blk=blk),
        grid=(blk.n_blocks,),
        in_specs=[
            pl.BlockSpec((HALO, 2 * CONV_CH), lambda t: (jnp.maximum(t * hb - 1, 0), 0)),
            pl.BlockSpec((TOKEN_BLOCK, 2 * CONV_CH), lambda t: (t, 0)),
            pl.BlockSpec((HALO, 2 * CONV_CH), lambda t: (jnp.minimum((t + 1) * hb, n_halo - 1), 0)),
            _resident((None, CONV_WIDTH, CONV_CH), lambda t: (l, 0, 0)),
            vec(), vec(), vec(),
        ],
        out_specs=pl.BlockSpec((TOKEN_BLOCK, CONV_CH), lambda t: (t, 0)),
        out_shape=jax.ShapeDtypeStruct((n, CONV_CH), F32),
        scratch_shapes=[pltpu.VMEM((TOKEN_BLOCK + 2 * HALO, CONV_CH), F32),
                        pltpu.VMEM((SUBLANES - 1, CONV_ROWS, CONV_CH), F32)],
        compiler_params=_params(1),
        name=f"conv_module_{l}",
    )(cu, cu, cu, conv_w, conv_b, ln_g, ln_b)


def _rope(x, cos, sin_signed, first_half):
    partner = jnp.where(first_half, pltpu.roll(x, LANES - ROPE_FREQS, 1), pltpu.roll(x, ROPE_FREQS, 1))
    return x * cos + partner * sin_signed


def _diff_kernel(*refs, rope, lam_init, n_self, n_cache, tq):
    if rope:
        (q_ref, k_ref, v_ref, ck_ref, cv_ref, cosq_ref, sinq_ref, cosk_ref, sink_ref, lam_ref, g_ref,
         _prev_ref, o_ref, k_scr, v_scr) = refs
    else:
        q_ref, k_ref, v_ref, lam_ref, g_ref, o_ref, k_scr, v_scr = refs
    slab = 2 * DIFF_DH
    lane = lax.broadcasted_iota(jnp.int32, (1, slab), 1)
    first_half = (lane % (2 * ROPE_FREQS)) < ROPE_FREQS
    lo = lane < DIFF_DH

    @pl.when(pl.program_id(1) == 0)
    def _():
        rows = 256
        for r0 in range(0, n_self, rows):
            rs = slice(r0, r0 + rows)
            for h in range(N_HEADS):
                cs = slice(h * slab, (h + 1) * slab)
                kx = k_ref[rs, cs]
                if rope:
                    kx = _rope(kx, cosk_ref[rs, :], sink_ref[rs, :], first_half)
                k_scr[rs, cs] = kx.astype(BF16)
            for h in range(N_HEADS):
                v_scr[rs, (2 * h) * DIFF_DV:(2 * h + 1) * DIFF_DV] = v_ref[rs, h * DIFF_DV:(h + 1) * DIFF_DV].astype(BF16)
                v_scr[rs, (2 * h + 1) * DIFF_DV:(2 * h + 2) * DIFF_DV] = jnp.ones((rows, DIFF_DV), BF16)
        if n_cache:
            cs_rows = slice(n_self, n_self + n_cache)
            k_scr[cs_rows, :] = ck_ref[...].astype(BF16)
            for h in range(N_HEADS):
                v_scr[cs_rows, (2 * h) * DIFF_DV:(2 * h + 1) * DIFF_DV] = cv_ref[:, h * DIFF_DV:(h + 1) * DIFF_DV].astype(BF16)
                v_scr[cs_rows, (2 * h + 1) * DIFF_DV:(2 * h + 2) * DIFF_DV] = jnp.ones((n_cache, DIFF_DV), BF16)

    lv = lam_ref[...]
    lam = (jnp.exp(jnp.sum(lv[0:1] * lv[1:2], axis=-1, keepdims=True))
           - jnp.exp(jnp.sum(lv[2:3] * lv[3:4], axis=-1, keepdims=True)) + lam_init)
    scale = DIFF_DH ** -0.5
    for h in range(N_HEADS):
        cs = slice(h * slab, (h + 1) * slab)
        qh = q_ref[:, cs]
        if rope:
            qh = _rope(qh, cosq_ref[...], sinq_ref[...], first_half)
        qh = qh * scale
        lhs = jnp.concatenate([jnp.where(lo, qh, 0.0), jnp.where(lo, 0.0, qh)], axis=0).astype(BF16)
        s = _dot_nt(lhs, k_scr[:, cs])
        p = jnp.exp(s - jnp.max(s, axis=-1, keepdims=True)).astype(BF16)
        pv = _dot(p, v_scr[:, (2 * h) * DIFF_DV:(2 * h + 2) * DIFF_DV])
        o = (pv[0:tq, 0:DIFF_DV] * (1.0 / pv[0:tq, DIFF_DV:])
             - pv[tq:, 0:DIFF_DV] * (lam / pv[tq:, DIFF_DV:]))
        ms = jnp.mean(o * o, axis=-1, keepdims=True)
        o_ref[:, h * DIFF_DV:(h + 1) * DIFF_DV] = o * lax.rsqrt(ms + EPS) * g_ref[...] * (1.0 - lam_init)


def _diff_attention(dq, dk, dv, cache_k, cache_v, rope_tabs, diff_lambda, subln_g, l, blk, out_like):
    n = dq.shape[0]
    lam_init = 0.8 - 0.6 * math.exp(-0.3 * l)
    n_p = blk.n_pb * TOKEN_BLOCK
    ls = blk.bps * TOKEN_BLOCK
    lam_spec = lambda: pl.BlockSpec((None, 4, DIFF_DH), lambda s, i: (l, 0, 0))
    g_spec = lambda: pl.BlockSpec((None, 1, DIFF_DV), lambda s, i: (l, 0, 0))

    o_prompt = pl.pallas_call(
        functools.partial(_diff_kernel, rope=False, lam_init=lam_init, n_self=TOKEN_BLOCK, n_cache=0, tq=TOKEN_BLOCK),
        grid=(blk.n_pb, 1),
        in_specs=[pl.BlockSpec((TOKEN_BLOCK, DIFF_QW), lambda s, i: (s, 0)),
                  pl.BlockSpec((TOKEN_BLOCK, DIFF_QW), lambda s, i: (s, 0)),
                  pl.BlockSpec((TOKEN_BLOCK, DIFF_VW), lambda s, i: (s, 0)),
                  lam_spec(), g_spec()],
        out_specs=pl.BlockSpec((TOKEN_BLOCK, DIFF_VW), lambda s, i: (s, 0)),
        out_shape=jax.ShapeDtypeStruct((n, DIFF_VW), F32),
        scratch_shapes=[pltpu.VMEM((TOKEN_BLOCK, DIFF_QW), BF16), pltpu.VMEM((TOKEN_BLOCK, 2 * DIFF_VW), BF16)],
        compiler_params=_params(2),
        name=f"diff_attn_prompt_{l}",
    )(dq, dk, dv, diff_lambda, subln_g)

    tq = TOKEN_BLOCK
    n_cache = cache_k.shape[2]
    cos_t, sin_t = rope_tabs
    seq0 = n_p // ls
    assert seq0 * ls == n_p
    q0 = n_p // tq
    o_all = pl.pallas_call(
        functools.partial(_diff_kernel, rope=True, lam_init=lam_init, n_self=ls, n_cache=n_cache, tq=tq),
        grid=(blk.n_ss, ls // tq),
        in_specs=[pl.BlockSpec((tq, DIFF_QW), lambda s, i: (q0 + s * (ls // tq) + i, 0)),
                  pl.BlockSpec((ls, DIFF_QW), lambda s, i: (seq0 + s, 0)),
                  pl.BlockSpec((ls, DIFF_VW), lambda s, i: (seq0 + s, 0)),
                  pl.BlockSpec((None, None, n_cache, DIFF_QW), lambda s, i: (s, l, 0, 0)),
                  pl.BlockSpec((None, None, n_cache, DIFF_VW), lambda s, i: (s, l, 0, 0)),
                  pl.BlockSpec((tq, LANES), lambda s, i: (i, 0)),
                  pl.BlockSpec((tq, LANES), lambda s, i: (i, 0)),
                  _resident((ls, LANES), lambda s, i: (0, 0)),
                  _resident((ls, LANES), lambda s, i: (0, 0)),
                  lam_spec(), g_spec(),
                  pl.BlockSpec(memory_space=pl.ANY)],
        out_specs=pl.BlockSpec((tq, DIFF_VW), lambda s, i: (q0 + s * (ls // tq) + i, 0)),
        out_shape=jax.ShapeDtypeStruct((n, DIFF_VW), F32),
        scratch_shapes=[pltpu.VMEM((ls + n_cache, DIFF_QW), BF16), pltpu.VMEM((ls + n_cache, 2 * DIFF_VW), BF16)],
        input_output_aliases={11: 0},
        compiler_params=_params(2),
        name=f"diff_attn_latent_{l}",
    )(dq, dk, dv, cache_k, cache_v, cos_t, sin_t, cos_t, sin_t, diff_lambda, subln_g, o_prompt)
    return o_all


def _post_kernel(x_ref, of_ref, ob_ref, g_ref, oc_ref, od_ref, gt_ref, g1_ref, sh2_ref, sc2_ref, g2_ref,
                 gng_ref, n2g_ref, fng_ref, wb_ref, wo_ref, wfi_ref, wfo_ref, out_ref, acc_ref, *, final):
    o = of_ref[...] + ob_ref[...]
    parts = []
    for h in range(N_HEADS):
        cs = slice(h * GLA_DV, (h + 1) * GLA_DV)
        oh = o[:, cs]
        ms = jnp.mean(oh * oh, axis=-1, keepdims=True)
        parts.append((oh * lax.rsqrt(ms + EPS) * gng_ref[...] * _silu(g_ref[:, cs])).astype(BF16))
    o_gla = jnp.concatenate(parts, axis=1)
    o_conv = oc_ref[...].astype(BF16)
    o_diff = od_ref[...].astype(BF16)
    b1 = GLA_VW
    b2 = b1 + CONV_CH
    b3 = b2 + DIFF_VW
    half = D_MODEL // 2
    merged = []
    for c0 in range(0, D_MODEL, half):
        cs = slice(c0, c0 + half)
        m = gt_ref[:, c0:c0 + half] * _dot(o_gla, wb_ref[0:b1, cs])
        m = m + gt_ref[:, D_MODEL + c0:D_MODEL + c0 + half] * _dot(o_conv, wb_ref[b1:b2, cs])
        m = m + gt_ref[:, 2 * D_MODEL + c0:2 * D_MODEL + c0 + half] * _dot(o_diff, wb_ref[b2:b3, cs])
        merged.append(m.astype(BF16))
    merged = jnp.concatenate(merged, axis=1)
    x1 = x_ref[...] + g1_ref[...] * _dot(merged, wo_ref[...])

    ms = jnp.mean(x1 * x1, axis=-1, keepdims=True)
    h2 = x1 * lax.rsqrt(ms + EPS) * n2g_ref[...]
    h2 = (h2 * (1.0 + sc2_ref[...]) + sh2_ref[...]).astype(BF16)
    for i, c0 in enumerate(range(0, FFN_HIDDEN, FFN_CHUNK)):
        a = _dot(h2, wfi_ref[:, c0:c0 + FFN_CHUNK])
        b = _dot(h2, wfi_ref[:, FFN_HIDDEN + c0:FFN_HIDDEN + c0 + FFN_CHUNK])
        u = (_silu(a) * b).astype(BF16)
        y = _dot(u, wfo_ref[c0:c0 + FFN_CHUNK, :])
        if i == 0:
            acc_ref[...] = y
        else:
            acc_ref[...] += y
    x2 = x1 + g2_ref[...] * acc_ref[...]
    if final:
        ms = jnp.mean(x2 * x2, axis=-1, keepdims=True)
        out_ref[...] = x2 * lax.rsqrt(ms + EPS) * fng_ref[...]
    else:
        out_ref[...] = x2


def _post(x, o_f, o_b, g, o_conv, o_diff, gates, mod4, gla_norm_g, norm2_g, final_g, wb, wo, wfi, wfo, l, blk, final):
    n = x.shape[0]
    row = blk.cond_row

    def tok(width):
        return pl.BlockSpec((TOKEN_BLOCK, width), lambda t: (t, 0))

    def mod(j):
        return pl.BlockSpec((None, None, 1, D_MODEL), lambda t: (l, row(t), 0, j))

    return pl.pallas_call(
        functools.partial(_post_kernel, final=final),
        grid=(blk.n_blocks,),
        in_specs=[
            tok(D_MODEL), tok(GLA_VW), tok(GLA_VW), tok(GLA_VW), tok(CONV_CH), tok(DIFF_VW), tok(N_BRANCH * D_MODEL),
            mod(2), mod(3), mod(4), mod(5),
            _resident((None, 1, GLA_DV), lambda t: (l, 0, 0)),
            _resident((None, 1, D_MODEL), lambda t: (l, 0, 0)),
            _resident((1, D_MODEL), lambda t: (0, 0)),
            _resident((None, GLA_VW + CONV_CH + DIFF_VW, D_MODEL), lambda t: (l, 0, 0)),
            _resident((None, D_MODEL, D_MODEL), lambda t: (l, 0, 0)),
            _resident((None, D_MODEL, 2 * FFN_HIDDEN), lambda t: (l, 0, 0)),
            _resident((None, FFN_HIDDEN, D_MODEL), lambda t: (l, 0, 0)),
        ],
        out_specs=tok(D_MODEL),
        out_shape=jax.ShapeDtypeStruct((n, D_MODEL), F32),
        scratch_shapes=[pltpu.VMEM((TOKEN_BLOCK, D_MODEL), F32)],
        compiler_params=_params(1),
        name=f"merge_ffn_{l}",
    )(x, o_f, o_b, g, o_conv, o_diff, gates, mod4, mod4, mod4, mod4, gla_norm_g, norm2_g, final_g, wb, wo, wfi, wfo)


def _rope_tables(n_tokens):
    rows = n_tokens // GRID_W
    row = jnp.repeat(jnp.arange(rows, dtype=F32), GRID_W)
    col = jnp.tile(jnp.arange(GRID_W, dtype=F32), rows)
    inv = ROPE_BASE ** (-jnp.arange(ROPE_FREQS, dtype=F32) / ROPE_FREQS)
    ang = jnp.stack([row[:, None] * inv, col[:, None] * inv], axis=1)
    cos, sin = jnp.cos(ang), jnp.sin(ang)
    cos_g = jnp.concatenate([cos, cos], axis=-1).reshape(n_tokens, 4 * ROPE_FREQS)
    sin_g = jnp.concatenate([-sin, sin], axis=-1).reshape(n_tokens, 4 * ROPE_FREQS)
    return jnp.tile(cos_g, (1, 2)), jnp.tile(sin_g, (1, 2))


def kernel(x_prompt, x_sample, cache_diff_k, cache_diff_v, state_gla, c, c_ctx, w_mod, b_mod, norm1_g, norm2_g,
           w_in, gla_w_a2, gla_b_a, gla_norm_g, conv_w, conv_b, conv_ln_g, conv_ln_b, diff_lambda, diff_subln_g,
           w_branch, w_out, w_ffn_in, w_ffn_out, final_norm_g):
    batch, seq, d = x_prompt.shape
    dec_batch, dec_seq, _ = x_sample.shape
    depth = w_mod.shape[0]
    assert d == D_MODEL and seq == TOKEN_BLOCK and dec_seq % TOKEN_BLOCK == 0 and dec_batch + 1 <= COND_ROWS
    blk = _Blocks(batch, dec_batch, dec_seq // TOKEN_BLOCK)
    n_p = batch * seq

    x = jnp.concatenate([x_prompt.reshape(n_p, d), x_sample.reshape(dec_batch * dec_seq, d)], axis=0)
    cond = jnp.concatenate([c_ctx[None, :], c, jnp.zeros((COND_ROWS - 1 - dec_batch, d), F32)], axis=0)
    assert w_in.shape[-1] == OFFS[10]
    w_in_p = _pack_in_weight(w_in)
    wa_p = jnp.zeros((depth, LR_PAD, 2 * GLA_QW), F32)
    wa_p = wa_p.at[:, 0:GLA_RANK, 0:GLA_QW].set(gla_w_a2[:, 0])
    wa_p = wa_p.at[:, GLA_RANK:2 * GLA_RANK, GLA_QW:].set(gla_w_a2[:, 1]).astype(BF16)
    ba_p = gla_b_a.reshape(depth, 1, 2 * GLA_QW)
    wb = w_branch.astype(BF16)
    wo = w_out.astype(BF16)
    wfi = w_ffn_in.astype(BF16)
    wfo = w_ffn_out.astype(BF16)
    vec = lambda a: a.reshape(depth, 1, a.shape[-1])
    rope_tabs = _rope_tables(dec_seq)
    ck = cache_diff_k.reshape(dec_batch, depth, -1, DIFF_QW)
    cv = cache_diff_v.reshape(dec_batch, depth, -1, DIFF_VW)
    hk = N_HEADS * GLA_DK
    s0_all = jnp.concatenate(
        [jnp.zeros((batch, depth, 2, hk, GLA_DV), F32), state_gla.reshape(dec_batch, depth, 2, hk, GLA_DV)], axis=0)

    mod4 = _modulation(cond, w_mod, b_mod).reshape(depth, COND_ROWS, 1, 6 * d)

    new_k, new_v, new_s = [], [], []
    for l in range(depth):
        q, k, v, g, la, cu, dq, dk, dv, gates = _in_projection(x, mod4, vec(norm1_g), w_in_p, wa_p, ba_p, l, blk)
        o_f, o_b, s_out = _gla(q, k, v, la, s0_all[:, l], blk)
        o_conv = _conv(cu, conv_w, vec(conv_b), vec(conv_ln_g), vec(conv_ln_b), l, blk)
        o_diff = _diff_attention(dq, dk, dv, ck, cv, rope_tabs, diff_lambda, vec(diff_subln_g), l, blk, x)
        x = _post(x, o_f, o_b, g, o_conv, o_diff, gates, mod4, vec(gla_norm_g), vec(norm2_g),
                  final_norm_g.reshape(1, d), wb, wo, wfi, wfo, l, blk, final=(l == depth - 1))
        new_k.append(dk[:n_p].reshape(batch, seq, N_HEADS, 2, DIFF_DH))
        new_v.append(dv[:n_p].reshape(batch, seq, N_HEADS, DIFF_DV))
        new_s.append(s_out[:batch].reshape(batch, 2, N_HEADS, GLA_DK, GLA_DV))
    y_prompt = x[:n_p].reshape(batch, seq, d)
    y_sample = x[n_p:].reshape(dec_batch, dec_seq, d)
    return (y_prompt, y_sample, jnp.stack(new_k, axis=1), jnp.stack(new_v, axis=1), jnp.stack(new_s, axis=1))
```

```python
import functools
import math

import jax
import jax.numpy as jnp
from jax import lax
from jax.experimental import pallas as pl
from jax.experimental.pallas import tpu as pltpu

F32 = jnp.float32
BF16 = jnp.bfloat16

D_MODEL = 1024
N_HEADS = 4
GLA_DK = 64
GLA_DV = 128
GLA_RANK = 16
GLA_TEMP = 16.0
GLA_CHUNK = 64
CONV_CH = 512
CONV_WIDTH = 31
CONV_PAD = (CONV_WIDTH - 1) // 2
DIFF_DH = 64
DIFF_DV = 128
ROPE_FREQS = DIFF_DH // 4
ROPE_BASE = 10000.0
GRID_W = 64
FFN_HIDDEN = -(-8 * D_MODEL // (3 * 256)) * 256
EPS = 1e-6
N_BRANCH = 3

GLA_QW = N_HEADS * GLA_DK
GLA_VW = N_HEADS * GLA_DV
DIFF_QW = N_HEADS * 2 * DIFF_DH
DIFF_VW = N_HEADS * DIFF_DV
SPLITS = (GLA_QW, GLA_QW, GLA_VW, GLA_VW, 2 * GLA_RANK, 2 * CONV_CH, DIFF_QW, DIFF_QW, DIFF_VW, N_BRANCH * D_MODEL)
OFFS = tuple(sum(SPLITS[:i]) for i in range(len(SPLITS) + 1))

TOKEN_BLOCK = 256
MM_BLOCK = 512
FFN_BLOCK = 1024
LANES = 128
SUBLANES = 8
HALO = 16
CONV_ROWS = TOKEN_BLOCK + 2 * HALO - SUBLANES
COND_ROWS = 8
LR_PAD = LANES
FFN_CHUNK = 256
VMEM_LIMIT = 60 * 1024 * 1024

C_Q, C_K, C_V, C_G = 0, GLA_QW, 2 * GLA_QW, 2 * GLA_QW + GLA_VW
C_CU = C_G + GLA_VW
C_DQ = C_CU + 2 * CONV_CH
C_DK = C_DQ + DIFF_QW
C_DV = C_DK + DIFF_QW
C_GT = C_DV + DIFF_VW
C_LR = C_GT + N_BRANCH * D_MODEL
IN_PACKED = C_LR + LR_PAD


def _params(n_axes=1):
    return pltpu.CompilerParams(dimension_semantics=("arbitrary",) * n_axes, vmem_limit_bytes=VMEM_LIMIT)


def _resident(shape, index_map):
    return pl.BlockSpec(shape, index_map, pipeline_mode=pl.Buffered(1))


def _silu(x):
    return x * jax.nn.sigmoid(x)


def _dot(a, b):
    return jnp.dot(a, b, preferred_element_type=F32)


def _dot_nt(a, b):
    return lax.dot_general(a, b, (((1,), (1,)), ((), ())), preferred_element_type=F32)


def _dot_tn(a, b):
    return lax.dot_general(a, b, (((0,), (0,)), ((), ())), preferred_element_type=F32)


class _Blocks:
    def __init__(self, n_pb, n_ss, bps):
        self.n_pb, self.n_ss, self.bps = n_pb, n_ss, bps
        self.n_blocks = n_pb + n_ss * bps
        self.n_seq = n_pb + n_ss

    def is_prompt(self, t):
        return t < self.n_pb

    def cond_row(self, t):
        return jnp.where(t < self.n_pb, 0, 1 + (t - self.n_pb) // self.bps)

    def seq(self, t):
        return jnp.where(t < self.n_pb, t, self.n_pb + (t - self.n_pb) // self.bps)

    def pos(self, t):
        return jnp.where(t < self.n_pb, 0, (t - self.n_pb) % self.bps)

    def last_pos(self, t):
        return jnp.where(t < self.n_pb, 0, self.bps - 1)

    def rev(self, t):
        u = t - self.n_pb
        return jnp.where(t < self.n_pb, t, self.n_pb + (u // self.bps) * self.bps + (self.bps - 1 - u % self.bps))


def _mod_kernel(c_ref, w_ref, b_ref, o_ref):
    s = _silu(c_ref[...]).astype(BF16)
    o_ref[...] = _dot(s, w_ref[...].astype(BF16)) + b_ref[...]


def _modulation(cond, w_mod, b_mod):
    depth = w_mod.shape[0]
    tn = 1536
    return pl.pallas_call(
        _mod_kernel,
        grid=(depth, 6 * D_MODEL // tn),
        in_specs=[
            pl.BlockSpec((COND_ROWS, D_MODEL), lambda l, j: (0, 0)),
            pl.BlockSpec((None, D_MODEL, tn), lambda l, j: (l, 0, j)),
            pl.BlockSpec((None, 1, tn), lambda l, j: (l, 0, j)),
        ],
        out_specs=pl.BlockSpec((None, COND_ROWS, tn), lambda l, j: (l, 0, j)),
        out_shape=jax.ShapeDtypeStruct((depth, COND_ROWS, 6 * D_MODEL), F32),
        compiler_params=_params(2),
        name="modulation",
    )(cond, w_mod, b_mod.reshape(depth, 1, 6 * D_MODEL))


def _pack_kernel(w_ref, o_ref):
    w = w_ref[...]
    o_ref[:, 0:C_CU] = w[:, OFFS[0]:OFFS[4]].astype(BF16)
    o_ref[:, C_CU:C_LR] = w[:, OFFS[5]:OFFS[10]].astype(BF16)
    lane = lax.broadcasted_iota(jnp.int32, (1, LR_PAD), 1)
    o_ref[:, C_LR:IN_PACKED] = jnp.where(lane < 2 * GLA_RANK, w[:, OFFS[4]:OFFS[4] + LR_PAD], 0.0).astype(BF16)


def _pack_in_weight(w_in):
    depth, d, width = w_in.shape
    rows = 256
    return pl.pallas_call(
        _pack_kernel,
        grid=(depth, d // rows),
        in_specs=[pl.BlockSpec((None, rows, width), lambda l, i: (l, i, 0))],
        out_specs=pl.BlockSpec((None, rows, IN_PACKED), lambda l, i: (l, i, 0)),
        out_shape=jax.ShapeDtypeStruct((depth, d, IN_PACKED), BF16),
        compiler_params=_params(2),
        name="pack_in_weight",
    )(w_in)


def _in_kernel(x_ref, sh_ref, sc_ref, ng_ref, w_ref, wa_ref, ba_ref,
               q_ref, k_ref, v_ref, g_ref, la_ref, cu_ref, dq_ref, dk_ref, dv_ref, gt_ref):
    x = x_ref[...]
    ms = jnp.mean(x * x, axis=-1, keepdims=True)
    h = x * lax.rsqrt(ms + EPS) * ng_ref[...]
    h = (h * (1.0 + sc_ref[...]) + sh_ref[...]).astype(BF16)

    def proj(c0, width, out_ref, fn=None):
        step = 512
        for a in range(0, width, step):
            b = min(a + step, width)
            r = _dot(h, w_ref[:, c0 + a:c0 + b])
            out_ref[:, a:b] = r if fn is None else fn(r)

    proj(C_Q, GLA_QW, q_ref)
    proj(C_K, GLA_QW, k_ref)
    proj(C_V, GLA_VW, v_ref)
    proj(C_G, GLA_VW, g_ref)
    proj(C_CU, 2 * CONV_CH, cu_ref)
    proj(C_DQ, DIFF_QW, dq_ref)
    proj(C_DK, DIFF_QW, dk_ref)
    proj(C_DV, DIFF_VW, dv_ref)
    proj(C_GT, N_BRANCH * D_MODEL, gt_ref, jax.nn.sigmoid)
    lr = _dot(h, w_ref[:, C_LR:C_LR + LR_PAD]).astype(BF16)
    xa = _dot(lr, wa_ref[...]) + ba_ref[...]
    la_ref[...] = (jnp.minimum(xa, 0.0) - jnp.log1p(jnp.exp(-jnp.abs(xa)))) * (1.0 / GLA_TEMP)


def _in_projection(x, mod4, norm_g, w_in_p, wa_p, ba_p, l, blk):
    n = x.shape[0]
    row = blk.cond_row

    def tok(width):
        return pl.BlockSpec((MM_BLOCK, width), lambda t: (t, 0))

    widths = (GLA_QW, GLA_QW, GLA_VW, GLA_VW, 2 * GLA_QW, 2 * CONV_CH, DIFF_QW, DIFF_QW, DIFF_VW, N_BRANCH * D_MODEL)
    return pl.pallas_call(
        _in_kernel,
        grid=(blk.n_blocks,),
        in_specs=[
            tok(D_MODEL),
            pl.BlockSpec((None, None, 1, D_MODEL), lambda t: (l, row(t), 0, 0)),
            pl.BlockSpec((None, None, 1, D_MODEL), lambda t: (l, row(t), 0, 1)),
            _resident((None, 1, D_MODEL), lambda t: (l, 0, 0)),
            _resident((None, D_MODEL, IN_PACKED), lambda t: (l, 0, 0)),
            _resident((None, LR_PAD, 2 * GLA_QW), lambda t: (l, 0, 0)),
            _resident((None, 1, 2 * GLA_QW), lambda t: (l, 0, 0)),
        ],
        out_specs=[tok(w) for w in widths],
        out_shape=[jax.ShapeDtypeStruct((n, w), F32) for w in widths],
        compiler_params=_params(1),
        name=f"in_projection_{l}",
    )(x, mod4, mod4, norm_g, w_in_p, wa_p, ba_p)


def _gla_prepare(q_ref, k_ref, v_ref, la_ref, tri, causal, lo, order, mid, last):
    la = la_ref[...]
    hi = la.astype(BF16)
    r1 = la - hi.astype(F32)
    md = r1.astype(BF16)
    lw = (r1 - md.astype(F32)).astype(BF16)
    b = _dot(tri, hi) + _dot(tri, md) + _dot(tri, lw)
    q = q_ref[...] * (GLA_DK ** -0.5)
    k = k_ref[...]
    c = GLA_CHUNK
    pair = 2 * GLA_DK
    chunks = []
    for n in order:
        sl = slice(n * c, (n + 1) * c)
        bn = b[sl]
        bmid = bn[mid:mid + 1]
        blast = bn[last:last + 1]
        qn, kn = q[sl], k[sl]
        vn = v_ref[sl, :].astype(BF16)
        qe = qn * jnp.exp(bn)
        qt = qn * jnp.exp(bn - bmid)
        kt = (kn * jnp.exp(bmid - bn)).astype(BF16)
        ku = (kn * jnp.exp(blast - bn)).astype(BF16)
        decay = jnp.exp(jnp.broadcast_to(blast, (LANES, N_HEADS * GLA_DK)).T)
        qes, atts, incs = [], [], []
        for p in range(N_HEADS // 2):
            cs = slice(p * pair, (p + 1) * pair)
            qt_p, qe_p = qt[:, cs], qe[:, cs]
            lhs = jnp.concatenate([jnp.where(lo, qt_p, 0.0), jnp.where(lo, 0.0, qt_p)], axis=0).astype(BF16)
            att = _dot_nt(lhs, kt[:, cs])
            atts += [jnp.where(causal, att[0:c], 0.0).astype(BF16), jnp.where(causal, att[c:2 * c], 0.0).astype(BF16)]
            qes += [jnp.where(lo, qe_p, 0.0).astype(BF16), jnp.where(lo, 0.0, qe_p).astype(BF16)]
            uf = _dot_tn(ku[:, cs], vn[:, (2 * p) * GLA_DV:(2 * p + 2) * GLA_DV])
            incs += [uf[0:GLA_DK, 0:GLA_DV], uf[GLA_DK:2 * GLA_DK, GLA_DV:2 * GLA_DV]]
        chunks.append((sl, qes, atts, vn, decay, jnp.concatenate(incs, axis=0)))
    return chunks


def _gla_scan(chunks, s_scr, o_ref):
    pair = 2 * GLA_DK
    s = s_scr[...]
    for sl, qes, atts, vn, decay, inc in chunks:
        s_b16 = s.astype(BF16)
        for h in range(N_HEADS):
            s_p = s_b16[(h // 2) * pair:(h // 2 + 1) * pair]
            cs = slice(h * GLA_DV, (h + 1) * GLA_DV)
            o_ref[sl, cs] = _dot(qes[h], s_p) + _dot(atts[h], vn[:, cs])
        s = s * decay + inc
    s_scr[...] = s


def _gla_kernel(qf_ref, kf_ref, vf_ref, laf_ref, qb_ref, kb_ref, vb_ref, lab_ref, s0_ref,
                of_ref, ob_ref, sout_ref, sf_scr, sb_scr, *, blk):
    t = pl.program_id(0)
    pos = blk.pos(t)

    @pl.when(pos == 0)
    def _():
        sf_scr[...] = s0_ref[0]
        sb_scr[...] = s0_ref[1]

    tb, c = TOKEN_BLOCK, GLA_CHUNK
    shift = int(math.log2(c))
    row = lax.broadcasted_iota(jnp.int32, (tb, tb), 0)
    col = lax.broadcasted_iota(jnp.int32, (tb, tb), 1)
    same = jnp.right_shift(row, shift) == jnp.right_shift(col, shift)
    tri_f = jnp.where(same, jnp.where(col <= row, 1.0, 0.0), 0.0).astype(BF16)
    tri_b = jnp.where(same, jnp.where(col >= row, 1.0, 0.0), 0.0).astype(BF16)
    r64 = lax.broadcasted_iota(jnp.int32, (c, c), 0)
    c64 = lax.broadcasted_iota(jnp.int32, (c, c), 1)
    lo = lax.broadcasted_iota(jnp.int32, (c, 2 * GLA_DK), 1) < GLA_DK
    n_chunks = tb // c
    fwd = _gla_prepare(qf_ref, kf_ref, vf_ref, laf_ref, tri_f, c64 <= r64, lo,
                       tuple(range(n_chunks)), c // 2 - 1, c - 1)
    bwd = _gla_prepare(qb_ref, kb_ref, vb_ref, lab_ref, tri_b, c64 >= r64, lo,
                       tuple(reversed(range(n_chunks))), c // 2, 0)
    _gla_scan(fwd, sf_scr, of_ref)
    _gla_scan(bwd, sb_scr, ob_ref)

    @pl.when(pos == blk.last_pos(t))
    def _():
        sout_ref[0] = sf_scr[...]
        sout_ref[1] = sb_scr[...]


def _gla(q, k, v, la, s0, blk):
    n = q.shape[0]
    rev = blk.rev
    hk, hv = N_HEADS * GLA_DK, N_HEADS * GLA_DV

    def fwd(width, cb=0):
        return pl.BlockSpec((TOKEN_BLOCK, width), lambda t: (t, cb))

    def bwd(width, cb=0):
        return pl.BlockSpec((TOKEN_BLOCK, width), lambda t: (rev(t), cb))

    state_spec = pl.BlockSpec((None, 2, hk, GLA_DV), lambda t: (blk.seq(t), 0, 0, 0))
    return pl.pallas_call(
        functools.partial(_gla_kernel, blk=blk),
        grid=(blk.n_blocks,),
        in_specs=[fwd(hk), fwd(hk), fwd(hv), fwd(hk, 0), bwd(hk), bwd(hk), bwd(hv), bwd(hk, 1), state_spec],
        out_specs=[fwd(hv), bwd(hv), state_spec],
        out_shape=[jax.ShapeDtypeStruct((n, hv), F32), jax.ShapeDtypeStruct((n, hv), F32),
                   jax.ShapeDtypeStruct((blk.n_seq, 2, hk, GLA_DV), F32)],
        scratch_shapes=[pltpu.VMEM((hk, GLA_DV), F32), pltpu.VMEM((hk, GLA_DV), F32)],
        compiler_params=_params(1),
        name="gla_scan",
    )(q, k, v, la, q, k, v, la, s0)


def _conv_kernel(prev_ref, cur_ref, next_ref, w_ref, b_ref, lg_ref, lb_ref, o_ref, y_scr, ys_scr, *, blk):
    t = pl.program_id(0)
    pos = blk.pos(t)
    ch = CONV_CH

    def glu(u):
        return u[:, :ch] * jax.nn.sigmoid(u[:, ch:])

    y_scr[0:HALO, :] = jnp.where(pos == 0, 0.0, glu(prev_ref[...]))
    y_scr[HALO:HALO + TOKEN_BLOCK, :] = glu(cur_ref[...])
    y_scr[HALO + TOKEN_BLOCK:, :] = jnp.where(pos == blk.last_pos(t), 0.0, glu(next_ref[...]))

    for s in range(1, SUBLANES):
        ys_scr[s - 1] = y_scr[s:s + CONV_ROWS, :]

    rows = 32
    first = HALO - CONV_PAD
    for r0 in range(0, TOKEN_BLOCK, rows):
        acc = jnp.zeros((rows, ch), F32)
        for w in range(CONV_WIDTH):
            s, a = (first + w) % SUBLANES, r0 + (first + w) // SUBLANES * SUBLANES
            tap = y_scr[a:a + rows, :] if s == 0 else ys_scr[s - 1, a:a + rows, :]
            acc = acc + tap * w_ref[w:w + 1, :]
        y = acc + b_ref[...]
        mu = jnp.mean(y, axis=-1, keepdims=True)
        d = y - mu
        var = jnp.mean(d * d, axis=-1, keepdims=True)
        z = d * lax.rsqrt(var + EPS) * lg_ref[...] + lb_ref[...]
        o_ref[r0:r0 + rows, :] = _silu(z)


def _conv(cu, conv_w, conv_b, ln_g, ln_b, l, blk):
    n = cu.shape[0]
    hb = TOKEN_BLOCK // HALO
    n_halo = n // HALO
    vec = lambda: _resident((None, 1, CONV_CH), lambda t: (l, 0, 0))
    return pl.pallas_call(
        functools.partial(_conv_kernel, blk=blk),
        grid=(blk.n_blocks,),
        in_specs=[
            pl.BlockSpec((HALO, 2 * CONV_CH), lambda t: (jnp.maximum(t * hb - 1, 0), 0)),
            pl.BlockSpec((TOKEN_BLOCK, 2 * CONV_CH), lambda t: (t, 0)),
            pl.BlockSpec((HALO, 2 * CONV_CH), lambda t: (jnp.minimum((t + 1) * hb, n_halo - 1), 0)),
            _resident((None, CONV_WIDTH, CONV_CH), lambda t: (l, 0, 0)),
            vec(), vec(), vec(),
        ],
        out_specs=pl.BlockSpec((TOKEN_BLOCK, CONV_CH), lambda t: (t, 0)),
        out_shape=jax.ShapeDtypeStruct((n, CONV_CH), F32),
        scratch_shapes=[pltpu.VMEM((TOKEN_BLOCK + 2 * HALO, CONV_CH), F32),
                        pltpu.VMEM((SUBLANES - 1, CONV_ROWS, CONV_CH), F32)],
        compiler_params=_params(1),
        name=f"conv_module_{l}",
    )(cu, cu, cu, conv_w, conv_b, ln_g, ln_b)


def _rope(x, cos, sin_signed, first_half):
    partner = jnp.where(first_half, pltpu.roll(x, LANES - ROPE_FREQS, 1), pltpu.roll(x, ROPE_FREQS, 1))
    return x * cos + partner * sin_signed


def _diff_kernel(*refs, rope, lam_init, n_self, n_cache, tq):
    if rope:
        (q_ref, k_ref, v_ref, ck_ref, cv_ref, cosq_ref, sinq_ref, cosk_ref, sink_ref, lam_ref, g_ref,
         _prev_ref, o_ref, k_scr, v_scr) = refs
    else:
        q_ref, k_ref, v_ref, lam_ref, g_ref, o_ref, k_scr, v_scr = refs
    slab = 2 * DIFF_DH
    lane = lax.broadcasted_iota(jnp.int32, (1, slab), 1)
    first_half = (lane % (2 * ROPE_FREQS)) < ROPE_FREQS
    lo = lane < DIFF_DH

    @pl.when(pl.program_id(1) == 0)
    def _():
        rows = 256
        for r0 in range(0, n_self, rows):
            rs = slice(r0, r0 + rows)
            for h in range(N_HEADS):
                cs = slice(h * slab, (h + 1) * slab)
                kx = k_ref[rs, cs]
                if rope:
                    kx = _rope(kx, cosk_ref[rs, :], sink_ref[rs, :], first_half)
                k_scr[rs, cs] = kx.astype(BF16)
            for h in range(N_HEADS):
                v_scr[rs, (2 * h) * DIFF_DV:(2 * h + 1) * DIFF_DV] = v_ref[rs, h * DIFF_DV:(h + 1) * DIFF_DV].astype(BF16)
                v_scr[rs, (2 * h + 1) * DIFF_DV:(2 * h + 2) * DIFF_DV] = jnp.ones((rows, DIFF_DV), BF16)
        if n_cache:
            cs_rows = slice(n_self, n_self + n_cache)
            k_scr[cs_rows, :] = ck_ref[...].astype(BF16)
            for h in range(N_HEADS):
                v_scr[cs_rows, (2 * h) * DIFF_DV:(2 * h + 1) * DIFF_DV] = cv_ref[:, h * DIFF_DV:(h + 1) * DIFF_DV].astype(BF16)
                v_scr[cs_rows, (2 * h + 1) * DIFF_DV:(2 * h + 2) * DIFF_DV] = jnp.ones((n_cache, DIFF_DV), BF16)

    lv = lam_ref[...]
    lam = (jnp.exp(jnp.sum(lv[0:1] * lv[1:2], axis=-1, keepdims=True))
           - jnp.exp(jnp.sum(lv[2:3] * lv[3:4], axis=-1, keepdims=True)) + lam_init)
    scale = DIFF_DH ** -0.5
    for h in range(N_HEADS):
        cs = slice(h * slab, (h + 1) * slab)
        qh = q_ref[:, cs]
        if rope:
            qh = _rope(qh, cosq_ref[...], sinq_ref[...], first_half)
        qh = qh * scale
        lhs = jnp.concatenate([jnp.where(lo, qh, 0.0), jnp.where(lo, 0.0, qh)], axis=0).astype(BF16)
        s = _dot_nt(lhs, k_scr[:, cs])
        p = jnp.exp(s - jnp.max(s, axis=-1, keepdims=True)).astype(BF16)
        pv = _dot(p, v_scr[:, (2 * h) * DIFF_DV:(2 * h + 2) * DIFF_DV])
        o = (pv[0:tq, 0:DIFF_DV] * (1.0 / pv[0:tq, DIFF_DV:])
             - pv[tq:, 0:DIFF_DV] * (lam / pv[tq:, DIFF_DV:]))
        ms = jnp.mean(o * o, axis=-1, keepdims=True)
        o_ref[:, h * DIFF_DV:(h + 1) * DIFF_DV] = o * lax.rsqrt(ms + EPS) * g_ref[...] * (1.0 - lam_init)


def _diff_attention(dq, dk, dv, cache_k, cache_v, rope_tabs, diff_lambda, subln_g, l, blk, out_like):
    n = dq.shape[0]
    lam_init = 0.8 - 0.6 * math.exp(-0.3 * l)
    n_p = blk.n_pb * TOKEN_BLOCK
    ls = blk.bps * TOKEN_BLOCK
    lam_spec = lambda: pl.BlockSpec((None, 4, DIFF_DH), lambda s, i: (l, 0, 0))
    g_spec = lambda: pl.BlockSpec((None, 1, DIFF_DV), lambda s, i: (l, 0, 0))

    o_prompt = pl.pallas_call(
        functools.partial(_diff_kernel, rope=False, lam_init=lam_init, n_self=TOKEN_BLOCK, n_cache=0, tq=TOKEN_BLOCK),
        grid=(blk.n_pb, 1),
        in_specs=[pl.BlockSpec((TOKEN_BLOCK, DIFF_QW), lambda s, i: (s, 0)),
                  pl.BlockSpec((TOKEN_BLOCK, DIFF_QW), lambda s, i: (s, 0)),
                  pl.BlockSpec((TOKEN_BLOCK, DIFF_VW), lambda s, i: (s, 0)),
                  lam_spec(), g_spec()],
        out_specs=pl.BlockSpec((TOKEN_BLOCK, DIFF_VW), lambda s, i: (s, 0)),
        out_shape=jax.ShapeDtypeStruct((n, DIFF_VW), F32),
        scratch_shapes=[pltpu.VMEM((TOKEN_BLOCK, DIFF_QW), BF16), pltpu.VMEM((TOKEN_BLOCK, 2 * DIFF_VW), BF16)],
        compiler_params=_params(2),
        name=f"diff_attn_prompt_{l}",
    )(dq, dk, dv, diff_lambda, subln_g)

    tq = TOKEN_BLOCK
    n_cache = cache_k.shape[2]
    cos_t, sin_t = rope_tabs
    seq0 = n_p // ls
    assert seq0 * ls == n_p
    q0 = n_p // tq
    o_all = pl.pallas_call(
        functools.partial(_diff_kernel, rope=True, lam_init=lam_init, n_self=ls, n_cache=n_cache, tq=tq),
        grid=(blk.n_ss, ls // tq),
        in_specs=[pl.BlockSpec((tq, DIFF_QW), lambda s, i: (q0 + s * (ls // tq) + i, 0)),
                  pl.BlockSpec((ls, DIFF_QW), lambda s, i: (seq0 + s, 0)),
                  pl.BlockSpec((ls, DIFF_VW), lambda s, i: (seq0 + s, 0)),
                  pl.BlockSpec((None, None, n_cache, DIFF_QW), lambda s, i: (s, l, 0, 0)),
                  pl.BlockSpec((None, None, n_cache, DIFF_VW), lambda s, i: (s, l, 0, 0)),
                  pl.BlockSpec((tq, LANES), lambda s, i: (i, 0)),
                  pl.BlockSpec((tq, LANES), lambda s, i: (i, 0)),
                  _resident((ls, LANES), lambda s, i: (0, 0)),
                  _resident((ls, LANES), lambda s, i: (0, 0)),
                  lam_spec(), g_spec(),
                  pl.BlockSpec(memory_space=pl.ANY)],
        out_specs=pl.BlockSpec((tq, DIFF_VW), lambda s, i: (q0 + s * (ls // tq) + i, 0)),
        out_shape=jax.ShapeDtypeStruct((n, DIFF_VW), F32),
        scratch_shapes=[pltpu.VMEM((ls + n_cache, DIFF_QW), BF16), pltpu.VMEM((ls + n_cache, 2 * DIFF_VW), BF16)],
        input_output_aliases={11: 0},
        compiler_params=_params(2),
        name=f"diff_attn_latent_{l}",
    )(dq, dk, dv, cache_k, cache_v, cos_t, sin_t, cos_t, sin_t, diff_lambda, subln_g, o_prompt)
    return o_all


def _merge_kernel(x_ref, of_ref, ob_ref, g_ref, oc_ref, od_ref, gt_ref, g1_ref, gng_ref, wb_ref, wo_ref, out_ref):
    o = of_ref[...] + ob_ref[...]
    parts = []
    for h in range(N_HEADS):
        cs = slice(h * GLA_DV, (h + 1) * GLA_DV)
        oh = o[:, cs]
        ms = jnp.mean(oh * oh, axis=-1, keepdims=True)
        parts.append((oh * lax.rsqrt(ms + EPS) * gng_ref[...] * _silu(g_ref[:, cs])).astype(BF16))
    o_gla = jnp.concatenate(parts, axis=1)
    o_conv = oc_ref[...].astype(BF16)
    o_diff = od_ref[...].astype(BF16)
    b1 = GLA_VW
    b2 = b1 + CONV_CH
    b3 = b2 + DIFF_VW
    half = D_MODEL // 2
    merged = []
    for c0 in range(0, D_MODEL, half):
        cs = slice(c0, c0 + half)
        m = gt_ref[:, c0:c0 + half] * _dot(o_gla, wb_ref[0:b1, cs])
        m = m + gt_ref[:, D_MODEL + c0:D_MODEL + c0 + half] * _dot(o_conv, wb_ref[b1:b2, cs])
        m = m + gt_ref[:, 2 * D_MODEL + c0:2 * D_MODEL + c0 + half] * _dot(o_diff, wb_ref[b2:b3, cs])
        merged.append(m.astype(BF16))
    merged = jnp.concatenate(merged, axis=1)
    out_ref[...] = x_ref[...] + g1_ref[...] * _dot(merged, wo_ref[...])


def _merge(x, o_f, o_b, g, o_conv, o_diff, gates, mod4, gla_norm_g, wb, wo, l, blk):
    n = x.shape[0]

    def tok(width):
        return pl.BlockSpec((MM_BLOCK, width), lambda t: (t, 0))

    return pl.pallas_call(
        _merge_kernel,
        grid=(blk.n_blocks,),
        in_specs=[
            tok(D_MODEL), tok(GLA_VW), tok(GLA_VW), tok(GLA_VW), tok(CONV_CH), tok(DIFF_VW), tok(N_BRANCH * D_MODEL),
            pl.BlockSpec((None, None, 1, D_MODEL), lambda t: (l, blk.cond_row(t), 0, 2)),
            _resident((None, 1, GLA_DV), lambda t: (l, 0, 0)),
            _resident((None, GLA_VW + CONV_CH + DIFF_VW, D_MODEL), lambda t: (l, 0, 0)),
            _resident((None, D_MODEL, D_MODEL), lambda t: (l, 0, 0)),
        ],
        out_specs=tok(D_MODEL),
        out_shape=jax.ShapeDtypeStruct((n, D_MODEL), F32),
        compiler_params=_params(1),
        name=f"merge_out_{l}",
    )(x, o_f, o_b, g, o_conv, o_diff, gates, mod4, gla_norm_g, wb, wo)


def _ffn_kernel(x_ref, sh2_ref, sc2_ref, g2_ref, n2g_ref, fng_ref, wfi_ref, wfo_ref, out_ref, acc_ref, *, final):
    x1 = x_ref[...]
    ms = jnp.mean(x1 * x1, axis=-1, keepdims=True)
    h2 = x1 * lax.rsqrt(ms + EPS) * n2g_ref[...]
    h2 = (h2 * (1.0 + sc2_ref[...]) + sh2_ref[...]).astype(BF16)
    for i, c0 in enumerate(range(0, FFN_HIDDEN, FFN_CHUNK)):
        a = _dot(h2, wfi_ref[:, c0:c0 + FFN_CHUNK])
        b = _dot(h2, wfi_ref[:, FFN_HIDDEN + c0:FFN_HIDDEN + c0 + FFN_CHUNK])
        u = (_silu(a) * b).astype(BF16)
        y = _dot(u, wfo_ref[c0:c0 + FFN_CHUNK, :])
        if i == 0:
            acc_ref[...] = y
        else:
            acc_ref[...] += y
    x2 = x_ref[...] + g2_ref[...] * acc_ref[...]
    if final:
        ms = jnp.mean(x2 * x2, axis=-1, keepdims=True)
        out_ref[...] = x2 * lax.rsqrt(ms + EPS) * fng_ref[...]
    else:
        out_ref[...] = x2


def _ffn(x, mod4, norm2_g, final_g, wfi, wfo, l, blk, final):
    n = x.shape[0]

    def mod(j):
        return pl.BlockSpec((None, None, 1, D_MODEL), lambda t: (l, blk.cond_row(t), 0, j))

    return pl.pallas_call(
        functools.partial(_ffn_kernel, final=final),
        grid=(blk.n_blocks,),
        in_specs=[
            pl.BlockSpec((FFN_BLOCK, D_MODEL), lambda t: (t, 0)),
            mod(3), mod(4), mod(5),
            _resident((None, 1, D_MODEL), lambda t: (l, 0, 0)),
            _resident((1, D_MODEL), lambda t: (0, 0)),
            _resident((None, D_MODEL, 2 * FFN_HIDDEN), lambda t: (l, 0, 0)),
            _resident((None, FFN_HIDDEN, D_MODEL), lambda t: (l, 0, 0)),
        ],
        out_specs=pl.BlockSpec((FFN_BLOCK, D_MODEL), lambda t: (t, 0)),
        out_shape=jax.ShapeDtypeStruct((n, D_MODEL), F32),
        scratch_shapes=[pltpu.VMEM((FFN_BLOCK, D_MODEL), F32)],
        compiler_params=_params(1),
        name=f"ffn_{l}",
    )(x, mod4, mod4, mod4, norm2_g, final_g, wfi, wfo)


def _rope_tables(n_tokens):
    rows = n_tokens // GRID_W
    row = jnp.repeat(jnp.arange(rows, dtype=F32), GRID_W)
    col = jnp.tile(jnp.arange(GRID_W, dtype=F32), rows)
    inv = ROPE_BASE ** (-jnp.arange(ROPE_FREQS, dtype=F32) / ROPE_FREQS)
    ang = jnp.stack([row[:, None] * inv, col[:, None] * inv], axis=1)
    cos, sin = jnp.cos(ang), jnp.sin(ang)
    cos_g = jnp.concatenate([cos, cos], axis=-1).reshape(n_tokens, 4 * ROPE_FREQS)
    sin_g = jnp.concatenate([-sin, sin], axis=-1).reshape(n_tokens, 4 * ROPE_FREQS)
    return jnp.tile(cos_g, (1, 2)), jnp.tile(sin_g, (1, 2))


def kernel(x_prompt, x_sample, cache_diff_k, cache_diff_v, state_gla, c, c_ctx, w_mod, b_mod, norm1_g, norm2_g,
           w_in, gla_w_a2, gla_b_a, gla_norm_g, conv_w, conv_b, conv_ln_g, conv_ln_b, diff_lambda, diff_subln_g,
           w_branch, w_out, w_ffn_in, w_ffn_out, final_norm_g):
    batch, seq, d = x_prompt.shape
    dec_batch, dec_seq, _ = x_sample.shape
    depth = w_mod.shape[0]
    assert d == D_MODEL and seq == TOKEN_BLOCK and dec_seq % TOKEN_BLOCK == 0 and dec_batch + 1 <= COND_ROWS
    blk = _Blocks(batch, dec_batch, dec_seq // TOKEN_BLOCK)
    n_p = batch * seq
    assert n_p % FFN_BLOCK == 0 and dec_seq % FFN_BLOCK == 0 and FFN_BLOCK % MM_BLOCK == 0
    blk_mm = _Blocks(n_p // MM_BLOCK, dec_batch, dec_seq // MM_BLOCK)
    blk_ffn = _Blocks(n_p // FFN_BLOCK, dec_batch, dec_seq // FFN_BLOCK)

    x = jnp.concatenate([x_prompt.reshape(n_p, d), x_sample.reshape(dec_batch * dec_seq, d)], axis=0)
    cond = jnp.concatenate([c_ctx[None, :], c, jnp.zeros((COND_ROWS - 1 - dec_batch, d), F32)], axis=0)
    assert w_in.shape[-1] == OFFS[10]
    w_in_p = _pack_in_weight(w_in)
    wa_p = jnp.zeros((depth, LR_PAD, 2 * GLA_QW), F32)
    wa_p = wa_p.at[:, 0:GLA_RANK, 0:GLA_QW].set(gla_w_a2[:, 0])
    wa_p = wa_p.at[:, GLA_RANK:2 * GLA_RANK, GLA_QW:].set(gla_w_a2[:, 1]).astype(BF16)
    ba_p = gla_b_a.reshape(depth, 1, 2 * GLA_QW)
    wb = w_branch.astype(BF16)
    wo = w_out.astype(BF16)
    wfi = w_ffn_in.astype(BF16)
    wfo = w_ffn_out.astype(BF16)
    vec = lambda a: a.reshape(depth, 1, a.shape[-1])
    rope_tabs = _rope_tables(dec_seq)
    ck = cache_diff_k.reshape(dec_batch, depth, -1, DIFF_QW)
    cv = cache_diff_v.reshape(dec_batch, depth, -1, DIFF_VW)
    hk = N_HEADS * GLA_DK
    s0_all = jnp.concatenate(
        [jnp.zeros((batch, depth, 2, hk, GLA_DV), F32), state_gla.reshape(dec_batch, depth, 2, hk, GLA_DV)], axis=0)

    mod4 = _modulation(cond, w_mod, b_mod).reshape(depth, COND_ROWS, 1, 6 * d)

    new_k, new_v, new_s = [], [], []
    for l in range(depth):
        q, k, v, g, la, cu, dq, dk, dv, gates = _in_projection(x, mod4, vec(norm1_g), w_in_p, wa_p, ba_p, l, blk_mm)
        o_f, o_b, s_out = _gla(q, k, v, la, s0_all[:, l], blk)
        o_conv = _conv(cu, conv_w, vec(conv_b), vec(conv_ln_g), vec(conv_ln_b), l, blk)
        o_diff = _diff_attention(dq, dk, dv, ck, cv, rope_tabs, diff_lambda, vec(diff_subln_g), l, blk, x)
        x = _merge(x, o_f, o_b, g, o_conv, o_diff, gates, mod4, vec(gla_norm_g), wb, wo, l, blk_mm)
        x = _ffn(x, mod4, vec(norm2_g), final_norm_g.reshape(1, d), wfi, wfo, l, blk_ffn, final=(l == depth - 1))
        new_k.append(dk[:n_p].reshape(batch, seq, N_HEADS, 2, DIFF_DH))
        new_v.append(dv[:n_p].reshape(batch, seq, N_HEADS, DIFF_DV))
        new_s.append(s_out[:batch].reshape(batch, 2, N_HEADS, GLA_DK, GLA_DV))
    y_prompt = x[:n_p].reshape(batch, seq, d)
    y_sample = x[n_p:].reshape(dec_batch, dec_seq, d)
    return (y_prompt, y_sample, jnp.stack(new_k, axis=1), jnp.stack(new_v, axis=1), jnp.stack(new_s, axis=1))
```

```python
import functools
import math

import jax
import jax.numpy as jnp
from jax import lax
from jax.experimental import pallas as pl
from jax.experimental.pallas import tpu as pltpu

F32 = jnp.float32
BF16 = jnp.bfloat16

D_MODEL = 1024
N_HEADS = 4
GLA_DK = 64
GLA_DV = 128
GLA_RANK = 16
GLA_TEMP = 16.0
GLA_CHUNK = 64
CONV_CH = 512
CONV_WIDTH = 31
CONV_PAD = (CONV_WIDTH - 1) // 2
DIFF_DH = 64
DIFF_DV = 128
ROPE_FREQS = DIFF_DH // 4
ROPE_BASE = 10000.0
GRID_W = 64
FFN_HIDDEN = -(-8 * D_MODEL // (3 * 256)) * 256
EPS = 1e-6
N_BRANCH = 3

GLA_QW = N_HEADS * GLA_DK
GLA_VW = N_HEADS * GLA_DV
DIFF_QW = N_HEADS * 2 * DIFF_DH
DIFF_VW = N_HEADS * DIFF_DV
SPLITS = (GLA_QW, GLA_QW, GLA_VW, GLA_VW, 2 * GLA_RANK, 2 * CONV_CH, DIFF_QW, DIFF_QW, DIFF_VW, N_BRANCH * D_MODEL)
OFFS = tuple(sum(SPLITS[:i]) for i in range(len(SPLITS) + 1))

TOKEN_BLOCK = 256
MM_BLOCK = 512
FFN_BLOCK = 1024
LANES = 128
SUBLANES = 8
HALO = 16
CONV_ROWS = TOKEN_BLOCK + 2 * HALO - SUBLANES
COND_ROWS = 8
LR_PAD = LANES
FFN_CHUNK = 256
VMEM_LIMIT = 60 * 1024 * 1024

C_Q, C_K, C_V, C_G = 0, GLA_QW, 2 * GLA_QW, 2 * GLA_QW + GLA_VW
C_CU = C_G + GLA_VW
C_DQ = C_CU + 2 * CONV_CH
C_DK = C_DQ + DIFF_QW
C_DV = C_DK + DIFF_QW
C_GT = C_DV + DIFF_VW
C_LR = C_GT + N_BRANCH * D_MODEL


def _params(n_axes=1):
    return pltpu.CompilerParams(dimension_semantics=("arbitrary",) * n_axes, vmem_limit_bytes=VMEM_LIMIT)


def _resident(shape, index_map):
    return pl.BlockSpec(shape, index_map, pipeline_mode=pl.Buffered(1))


def _silu(x):
    return x * jax.nn.sigmoid(x)


def _dot(a, b):
    return jnp.dot(a, b, preferred_element_type=F32)


def _dot_nt(a, b):
    return lax.dot_general(a, b, (((1,), (1,)), ((), ())), preferred_element_type=F32)


def _dot_tn(a, b):
    return lax.dot_general(a, b, (((0,), (0,)), ((), ())), preferred_element_type=F32)


class _Blocks:
    def __init__(self, n_pb, n_ss, bps):
        self.n_pb, self.n_ss, self.bps = n_pb, n_ss, bps
        self.n_blocks = n_pb + n_ss * bps
        self.n_seq = n_pb + n_ss

    def is_prompt(self, t):
        return t < self.n_pb

    def cond_row(self, t):
        return jnp.where(t < self.n_pb, 0, 1 + (t - self.n_pb) // self.bps)

    def seq(self, t):
        return jnp.where(t < self.n_pb, t, self.n_pb + (t - self.n_pb) // self.bps)

    def pos(self, t):
        return jnp.where(t < self.n_pb, 0, (t - self.n_pb) % self.bps)

    def last_pos(self, t):
        return jnp.where(t < self.n_pb, 0, self.bps - 1)

    def rev(self, t):
        u = t - self.n_pb
        return jnp.where(t < self.n_pb, t, self.n_pb + (u // self.bps) * self.bps + (self.bps - 1 - u % self.bps))


def _mod_kernel(c_ref, w_ref, b_ref, o_ref):
    s = _silu(c_ref[...]).astype(BF16)
    o_ref[...] = _dot(s, w_ref[...].astype(BF16)) + b_ref[...]


def _modulation(cond, w_mod, b_mod):
    depth = w_mod.shape[0]
    tn = 1536
    return pl.pallas_call(
        _mod_kernel,
        grid=(depth, 6 * D_MODEL // tn),
        in_specs=[
            pl.BlockSpec((COND_ROWS, D_MODEL), lambda l, j: (0, 0)),
            pl.BlockSpec((None, D_MODEL, tn), lambda l, j: (l, 0, j)),
            pl.BlockSpec((None, 1, tn), lambda l, j: (l, 0, j)),
        ],
        out_specs=pl.BlockSpec((None, COND_ROWS, tn), lambda l, j: (l, 0, j)),
        out_shape=jax.ShapeDtypeStruct((depth, COND_ROWS, 6 * D_MODEL), F32),
        compiler_params=_params(2),
        name="modulation",
    )(cond, w_mod, b_mod.reshape(depth, 1, 6 * D_MODEL))


def _in_kernel(x_ref, sh_ref, sc_ref, ng_ref, wg_ref, wr_ref, wl_ref, wa_ref, ba_ref,
               q_ref, k_ref, v_ref, g_ref, la_ref, cu_ref, dq_ref, dk_ref, dv_ref, gt_ref):
    x = x_ref[...]
    ms = jnp.mean(x * x, axis=-1, keepdims=True)
    h = x * lax.rsqrt(ms + EPS) * ng_ref[...]
    h = (h * (1.0 + sc_ref[...]) + sh_ref[...]).astype(BF16)

    def proj(w_ref, c0, width, out_ref, fn=None):
        step = 512
        for a in range(0, width, step):
            b = min(a + step, width)
            r = _dot(h, w_ref[:, c0 + a:c0 + b])
            out_ref[:, a:b] = r if fn is None else fn(r)

    proj(wg_ref, C_Q, GLA_QW, q_ref)
    proj(wg_ref, C_K, GLA_QW, k_ref)
    proj(wg_ref, C_V, GLA_VW, v_ref)
    proj(wg_ref, C_G, GLA_VW, g_ref)
    proj(wr_ref, 0, 2 * CONV_CH, cu_ref)
    proj(wr_ref, C_DQ - C_CU, DIFF_QW, dq_ref)
    proj(wr_ref, C_DK - C_CU, DIFF_QW, dk_ref)
    proj(wr_ref, C_DV - C_CU, DIFF_VW, dv_ref)
    proj(wr_ref, C_GT - C_CU, N_BRANCH * D_MODEL, gt_ref, jax.nn.sigmoid)
    lr = _dot(h, wl_ref[...]).astype(BF16)
    xa = _dot(lr, wa_ref[...]) + ba_ref[...]
    la_ref[...] = (jnp.minimum(xa, 0.0) - jnp.log1p(jnp.exp(-jnp.abs(xa)))) * (1.0 / GLA_TEMP)


def _in_projection(x, mod4, norm_g, w_gla, w_rest, w_lr, wa_p, ba_p, l, blk):
    n = x.shape[0]
    row = blk.cond_row

    def tok(width):
        return pl.BlockSpec((MM_BLOCK, width), lambda t: (t, 0))

    widths = (GLA_QW, GLA_QW, GLA_VW, GLA_VW, 2 * GLA_QW, 2 * CONV_CH, DIFF_QW, DIFF_QW, DIFF_VW, N_BRANCH * D_MODEL)
    return pl.pallas_call(
        _in_kernel,
        grid=(blk.n_blocks,),
        in_specs=[
            tok(D_MODEL),
            pl.BlockSpec((None, None, 1, D_MODEL), lambda t: (l, row(t), 0, 0)),
            pl.BlockSpec((None, None, 1, D_MODEL), lambda t: (l, row(t), 0, 1)),
            _resident((None, 1, D_MODEL), lambda t: (l, 0, 0)),
            _resident((None, D_MODEL, C_CU), lambda t: (l, 0, 0)),
            _resident((None, D_MODEL, C_LR - C_CU), lambda t: (l, 0, 0)),
            _resident((None, D_MODEL, LR_PAD), lambda t: (l, 0, 0)),
            _resident((None, LR_PAD, 2 * GLA_QW), lambda t: (l, 0, 0)),
            _resident((None, 1, 2 * GLA_QW), lambda t: (l, 0, 0)),
        ],
        out_specs=[tok(w) for w in widths],
        out_shape=[jax.ShapeDtypeStruct((n, w), F32) for w in widths],
        compiler_params=_params(1),
        name=f"in_projection_{l}",
    )(x, mod4, mod4, norm_g, w_gla, w_rest, w_lr, wa_p, ba_p)


def _gla_prepare(q_ref, k_ref, v_ref, la_ref, tri, causal, lo, order, mid, last):
    la = la_ref[...]
    hi = la.astype(BF16)
    r1 = la - hi.astype(F32)
    md = r1.astype(BF16)
    lw = (r1 - md.astype(F32)).astype(BF16)
    b = _dot(tri, hi) + _dot(tri, md) + _dot(tri, lw)
    q = q_ref[...] * (GLA_DK ** -0.5)
    k = k_ref[...]
    c = GLA_CHUNK
    pair = 2 * GLA_DK
    chunks = []
    for n in order:
        sl = slice(n * c, (n + 1) * c)
        bn = b[sl]
        bmid = bn[mid:mid + 1]
        blast = bn[last:last + 1]
        qn, kn = q[sl], k[sl]
        vn = v_ref[sl, :].astype(BF16)
        qe = qn * jnp.exp(bn)
        qt = qn * jnp.exp(bn - bmid)
        kt = (kn * jnp.exp(bmid - bn)).astype(BF16)
        ku = (kn * jnp.exp(blast - bn)).astype(BF16)
        decay = jnp.exp(jnp.broadcast_to(blast, (LANES, N_HEADS * GLA_DK)).T)
        qes, atts, incs = [], [], []
        for p in range(N_HEADS // 2):
            cs = slice(p * pair, (p + 1) * pair)
            qt_p, qe_p = qt[:, cs], qe[:, cs]
            lhs = jnp.concatenate([jnp.where(lo, qt_p, 0.0), jnp.where(lo, 0.0, qt_p)], axis=0).astype(BF16)
            att = _dot_nt(lhs, kt[:, cs])
            atts += [jnp.where(causal, att[0:c], 0.0).astype(BF16), jnp.where(causal, att[c:2 * c], 0.0).astype(BF16)]
            qes += [jnp.where(lo, qe_p, 0.0).astype(BF16), jnp.where(lo, 0.0, qe_p).astype(BF16)]
            uf = _dot_tn(ku[:, cs], vn[:, (2 * p) * GLA_DV:(2 * p + 2) * GLA_DV])
            incs += [uf[0:GLA_DK, 0:GLA_DV], uf[GLA_DK:2 * GLA_DK, GLA_DV:2 * GLA_DV]]
        chunks.append((sl, qes, atts, vn, decay, jnp.concatenate(incs, axis=0)))
    return chunks


def _gla_scan(chunks, s_scr, o_ref):
    pair = 2 * GLA_DK
    s = s_scr[...]
    for sl, qes, atts, vn, decay, inc in chunks:
        s_b16 = s.astype(BF16)
        for h in range(N_HEADS):
            s_p = s_b16[(h // 2) * pair:(h // 2 + 1) * pair]
            cs = slice(h * GLA_DV, (h + 1) * GLA_DV)
            o_ref[sl, cs] = _dot(qes[h], s_p) + _dot(atts[h], vn[:, cs])
        s = s * decay + inc
    s_scr[...] = s


def _gla_kernel(qf_ref, kf_ref, vf_ref, laf_ref, qb_ref, kb_ref, vb_ref, lab_ref, s0_ref,
                of_ref, ob_ref, sout_ref, sf_scr, sb_scr, *, blk):
    t = pl.program_id(0)
    pos = blk.pos(t)

    @pl.when(pos == 0)
    def _():
        sf_scr[...] = s0_ref[0]
        sb_scr[...] = s0_ref[1]

    tb, c = TOKEN_BLOCK, GLA_CHUNK
    shift = int(math.log2(c))
    row = lax.broadcasted_iota(jnp.int32, (tb, tb), 0)
    col = lax.broadcasted_iota(jnp.int32, (tb, tb), 1)
    same = jnp.right_shift(row, shift) == jnp.right_shift(col, shift)
    tri_f = jnp.where(same, jnp.where(col <= row, 1.0, 0.0), 0.0).astype(BF16)
    tri_b = jnp.where(same, jnp.where(col >= row, 1.0, 0.0), 0.0).astype(BF16)
    r64 = lax.broadcasted_iota(jnp.int32, (c, c), 0)
    c64 = lax.broadcasted_iota(jnp.int32, (c, c), 1)
    lo = lax.broadcasted_iota(jnp.int32, (c, 2 * GLA_DK), 1) < GLA_DK
    n_chunks = tb // c
    fwd = _gla_prepare(qf_ref, kf_ref, vf_ref, laf_ref, tri_f, c64 <= r64, lo,
                       tuple(range(n_chunks)), c // 2 - 1, c - 1)
    bwd = _gla_prepare(qb_ref, kb_ref, vb_ref, lab_ref, tri_b, c64 >= r64, lo,
                       tuple(reversed(range(n_chunks))), c // 2, 0)
    _gla_scan(fwd, sf_scr, of_ref)
    _gla_scan(bwd, sb_scr, ob_ref)

    @pl.when(pos == blk.last_pos(t))
    def _():
        sout_ref[0] = sf_scr[...]
        sout_ref[1] = sb_scr[...]


def _gla(q, k, v, la, s0, blk):
    n = q.shape[0]
    rev = blk.rev
    hk, hv = N_HEADS * GLA_DK, N_HEADS * GLA_DV

    def fwd(width, cb=0):
        return pl.BlockSpec((TOKEN_BLOCK, width), lambda t: (t, cb))

    def bwd(width, cb=0):
        return pl.BlockSpec((TOKEN_BLOCK, width), lambda t: (rev(t), cb))

    state_spec = pl.BlockSpec((None, 2, hk, GLA_DV), lambda t: (blk.seq(t), 0, 0, 0))
    return pl.pallas_call(
        functools.partial(_gla_kernel, blk=blk),
        grid=(blk.n_blocks,),
        in_specs=[fwd(hk), fwd(hk), fwd(hv), fwd(hk, 0), bwd(hk), bwd(hk), bwd(hv), bwd(hk, 1), state_spec],
        out_specs=[fwd(hv), bwd(hv), state_spec],
        out_shape=[jax.ShapeDtypeStruct((n, hv), F32), jax.ShapeDtypeStruct((n, hv), F32),
                   jax.ShapeDtypeStruct((blk.n_seq, 2, hk, GLA_DV), F32)],
        scratch_shapes=[pltpu.VMEM((hk, GLA_DV), F32), pltpu.VMEM((hk, GLA_DV), F32)],
        compiler_params=_params(1),
        name="gla_scan",
    )(q, k, v, la, q, k, v, la, s0)


def _conv_kernel(prev_ref, cur_ref, next_ref, w_ref, b_ref, lg_ref, lb_ref, o_ref, y_scr, ys_scr, *, blk):
    t = pl.program_id(0)
    pos = blk.pos(t)
    ch = CONV_CH

    def glu(u):
        return u[:, :ch] * jax.nn.sigmoid(u[:, ch:])

    y_scr[0:HALO, :] = jnp.where(pos == 0, 0.0, glu(prev_ref[...]))
    y_scr[HALO:HALO + TOKEN_BLOCK, :] = glu(cur_ref[...])
    y_scr[HALO + TOKEN_BLOCK:, :] = jnp.where(pos == blk.last_pos(t), 0.0, glu(next_ref[...]))

    for s in range(1, SUBLANES):
        ys_scr[s - 1] = y_scr[s:s + CONV_ROWS, :]

    rows = 32
    first = HALO - CONV_PAD
    for r0 in range(0, TOKEN_BLOCK, rows):
        acc = jnp.zeros((rows, ch), F32)
        for w in range(CONV_WIDTH):
            s, a = (first + w) % SUBLANES, r0 + (first + w) // SUBLANES * SUBLANES
            tap = y_scr[a:a + rows, :] if s == 0 else ys_scr[s - 1, a:a + rows, :]
            acc = acc + tap * w_ref[w:w + 1, :]
        y = acc + b_ref[...]
        mu = jnp.mean(y, axis=-1, keepdims=True)
        d = y - mu
        var = jnp.mean(d * d, axis=-1, keepdims=True)
        z = d * lax.rsqrt(var + EPS) * lg_ref[...] + lb_ref[...]
        o_ref[r0:r0 + rows, :] = _silu(z)


def _conv(cu, conv_w, conv_b, ln_g, ln_b, l, blk):
    n = cu.shape[0]
    hb = TOKEN_BLOCK // HALO
    n_halo = n // HALO
    vec = lambda: _resident((None, 1, CONV_CH), lambda t: (l, 0, 0))
    return pl.pallas_call(
        functools.partial(_conv_kernel, blk=blk),
        grid=(blk.n_blocks,),
        in_specs=[
            pl.BlockSpec((HALO, 2 * CONV_CH), lambda t: (jnp.maximum(t * hb - 1, 0), 0)),
            pl.BlockSpec((TOKEN_BLOCK, 2 * CONV_CH), lambda t: (t, 0)),
            pl.BlockSpec((HALO, 2 * CONV_CH), lambda t: (jnp.minimum((t + 1) * hb, n_halo - 1), 0)),
            _resident((None, CONV_WIDTH, CONV_CH), lambda t: (l, 0, 0)),
            vec(), vec(), vec(),
        ],
        out_specs=pl.BlockSpec((TOKEN_BLOCK, CONV_CH), lambda t: (t, 0)),
        out_shape=jax.ShapeDtypeStruct((n, CONV_CH), F32),
        scratch_shapes=[pltpu.VMEM((TOKEN_BLOCK + 2 * HALO, CONV_CH), F32),
                        pltpu.VMEM((SUBLANES - 1, CONV_ROWS, CONV_CH), F32)],
        compiler_params=_params(1),
        name=f"conv_module_{l}",
    )(cu, cu, cu, conv_w, conv_b, ln_g, ln_b)


def _rope(x, cos, sin_signed, first_half):
    partner = jnp.where(first_half, pltpu.roll(x, LANES - ROPE_FREQS, 1), pltpu.roll(x, ROPE_FREQS, 1))
    return x * cos + partner * sin_signed


def _diff_kernel(*refs, rope, lam_init, n_self, n_cache, tq):
    if rope:
        (q_ref, k_ref, v_ref, ck_ref, cv_ref, cosq_ref, sinq_ref, cosk_ref, sink_ref, lam_ref, g_ref,
         o_ref, k_scr, v_scr) = refs
    else:
        q_ref, k_ref, v_ref, lam_ref, g_ref, o_ref, k_scr, v_scr = refs
    slab = 2 * DIFF_DH
    lane = lax.broadcasted_iota(jnp.int32, (1, slab), 1)
    first_half = (lane % (2 * ROPE_FREQS)) < ROPE_FREQS
    lo = lane < DIFF_DH

    @pl.when(pl.program_id(1) == 0)
    def _():
        rows = 256
        for r0 in range(0, n_self, rows):
            rs = slice(r0, r0 + rows)
            for h in range(N_HEADS):
                cs = slice(h * slab, (h + 1) * slab)
                kx = k_ref[rs, cs]
                if rope:
                    kx = _rope(kx, cosk_ref[rs, :], sink_ref[rs, :], first_half)
                k_scr[rs, cs] = kx.astype(BF16)
            for h in range(N_HEADS):
                v_scr[rs, (2 * h) * DIFF_DV:(2 * h + 1) * DIFF_DV] = v_ref[rs, h * DIFF_DV:(h + 1) * DIFF_DV].astype(BF16)
                v_scr[rs, (2 * h + 1) * DIFF_DV:(2 * h + 2) * DIFF_DV] = jnp.ones((rows, DIFF_DV), BF16)
        if n_cache:
            cs_rows = slice(n_self, n_self + n_cache)
            k_scr[cs_rows, :] = ck_ref[...].astype(BF16)
            for h in range(N_HEADS):
                v_scr[cs_rows, (2 * h) * DIFF_DV:(2 * h + 1) * DIFF_DV] = cv_ref[:, h * DIFF_DV:(h + 1) * DIFF_DV].astype(BF16)
                v_scr[cs_rows, (2 * h + 1) * DIFF_DV:(2 * h + 2) * DIFF_DV] = jnp.ones((n_cache, DIFF_DV), BF16)

    lv = lam_ref[...]
    lam = (jnp.exp(jnp.sum(lv[0:1] * lv[1:2], axis=-1, keepdims=True))
           - jnp.exp(jnp.sum(lv[2:3] * lv[3:4], axis=-1, keepdims=True)) + lam_init)
    scale = DIFF_DH ** -0.5
    for h in range(N_HEADS):
        cs = slice(h * slab, (h + 1) * slab)
        qh = q_ref[:, cs]
        if rope:
            qh = _rope(qh, cosq_ref[...], sinq_ref[...], first_half)
        qh = qh * scale
        lhs = jnp.concatenate([jnp.where(lo, qh, 0.0), jnp.where(lo, 0.0, qh)], axis=0).astype(BF16)
        s = _dot_nt(lhs, k_scr[:, cs])
        p = jnp.exp(s - jnp.max(s, axis=-1, keepdims=True)).astype(BF16)
        pv = _dot(p, v_scr[:, (2 * h) * DIFF_DV:(2 * h + 2) * DIFF_DV])
        o = (pv[0:tq, 0:DIFF_DV] * (1.0 / pv[0:tq, DIFF_DV:])
             - pv[tq:, 0:DIFF_DV] * (lam / pv[tq:, DIFF_DV:]))
        ms = jnp.mean(o * o, axis=-1, keepdims=True)
        o_ref[:, h * DIFF_DV:(h + 1) * DIFF_DV] = o * lax.rsqrt(ms + EPS) * g_ref[...] * (1.0 - lam_init)


def _diff_attention(dq, dk, dv, cache_k, cache_v, rope_tabs, diff_lambda, subln_g, l, blk):
    n = dq.shape[0]
    lam_init = 0.8 - 0.6 * math.exp(-0.3 * l)
    n_p = blk.n_pb * TOKEN_BLOCK
    ls = blk.bps * TOKEN_BLOCK
    lam_spec = lambda: pl.BlockSpec((None, 4, DIFF_DH), lambda s, i: (l, 0, 0))
    g_spec = lambda: pl.BlockSpec((None, 1, DIFF_DV), lambda s, i: (l, 0, 0))

    o_prompt = pl.pallas_call(
        functools.partial(_diff_kernel, rope=False, lam_init=lam_init, n_self=TOKEN_BLOCK, n_cache=0, tq=TOKEN_BLOCK),
        grid=(blk.n_pb, 1),
        in_specs=[pl.BlockSpec((TOKEN_BLOCK, DIFF_QW), lambda s, i: (s, 0)),
                  pl.BlockSpec((TOKEN_BLOCK, DIFF_QW), lambda s, i: (s, 0)),
                  pl.BlockSpec((TOKEN_BLOCK, DIFF_VW), lambda s, i: (s, 0)),
                  lam_spec(), g_spec()],
        out_specs=pl.BlockSpec((TOKEN_BLOCK, DIFF_VW), lambda s, i: (s, 0)),
        out_shape=jax.ShapeDtypeStruct((n_p, DIFF_VW), F32),
        scratch_shapes=[pltpu.VMEM((TOKEN_BLOCK, DIFF_QW), BF16), pltpu.VMEM((TOKEN_BLOCK, 2 * DIFF_VW), BF16)],
        compiler_params=_params(2),
        name=f"diff_attn_prompt_{l}",
    )(dq, dk, dv, diff_lambda, subln_g)

    tq = TOKEN_BLOCK
    n_cache = cache_k.shape[2]
    cos_t, sin_t = rope_tabs
    seq0 = n_p // ls
    assert seq0 * ls == n_p
    q0 = n_p // tq
    o_all = pl.pallas_call(
        functools.partial(_diff_kernel, rope=True, lam_init=lam_init, n_self=ls, n_cache=n_cache, tq=tq),
        grid=(blk.n_ss, ls // tq),
        in_specs=[pl.BlockSpec((tq, DIFF_QW), lambda s, i: (q0 + s * (ls // tq) + i, 0)),
                  pl.BlockSpec((ls, DIFF_QW), lambda s, i: (seq0 + s, 0)),
                  pl.BlockSpec((ls, DIFF_VW), lambda s, i: (seq0 + s, 0)),
                  pl.BlockSpec((None, None, n_cache, DIFF_QW), lambda s, i: (s, l, 0, 0)),
                  pl.BlockSpec((None, None, n_cache, DIFF_VW), lambda s, i: (s, l, 0, 0)),
                  pl.BlockSpec((tq, LANES), lambda s, i: (i, 0)),
                  pl.BlockSpec((tq, LANES), lambda s, i: (i, 0)),
                  _resident((ls, LANES), lambda s, i: (0, 0)),
                  _resident((ls, LANES), lambda s, i: (0, 0)),
                  lam_spec(), g_spec()],
        out_specs=pl.BlockSpec((tq, DIFF_VW), lambda s, i: (s * (ls // tq) + i, 0)),
        out_shape=jax.ShapeDtypeStruct((n - n_p, DIFF_VW), F32),
        scratch_shapes=[pltpu.VMEM((ls + n_cache, DIFF_QW), BF16), pltpu.VMEM((ls + n_cache, 2 * DIFF_VW), BF16)],
        compiler_params=_params(2),
        name=f"diff_attn_latent_{l}",
    )(dq, dk, dv, cache_k, cache_v, cos_t, sin_t, cos_t, sin_t, diff_lambda, subln_g)
    return o_prompt, o_all


def _merge_kernel(x_ref, of_ref, ob_ref, g_ref, oc_ref, odp_ref, ods_ref, gt_ref, g1_ref, gng_ref, wb_ref, wo_ref,
                  out_ref, *, n_prompt_blocks):
    o = of_ref[...] + ob_ref[...]
    parts = []
    for h in range(N_HEADS):
        cs = slice(h * GLA_DV, (h + 1) * GLA_DV)
        oh = o[:, cs]
        ms = jnp.mean(oh * oh, axis=-1, keepdims=True)
        parts.append((oh * lax.rsqrt(ms + EPS) * gng_ref[...] * _silu(g_ref[:, cs])).astype(BF16))
    o_gla = jnp.concatenate(parts, axis=1)
    o_conv = oc_ref[...].astype(BF16)
    o_diff = jnp.where(pl.program_id(0) < n_prompt_blocks, odp_ref[...], ods_ref[...]).astype(BF16)
    b1 = GLA_VW
    b2 = b1 + CONV_CH
    b3 = b2 + DIFF_VW
    half = D_MODEL // 2
    merged = []
    for c0 in range(0, D_MODEL, half):
        cs = slice(c0, c0 + half)
        m = gt_ref[:, c0:c0 + half] * _dot(o_gla, wb_ref[0:b1, cs])
        m = m + gt_ref[:, D_MODEL + c0:D_MODEL + c0 + half] * _dot(o_conv, wb_ref[b1:b2, cs])
        m = m + gt_ref[:, 2 * D_MODEL + c0:2 * D_MODEL + c0 + half] * _dot(o_diff, wb_ref[b2:b3, cs])
        merged.append(m.astype(BF16))
    merged = jnp.concatenate(merged, axis=1)
    out_ref[...] = x_ref[...] + g1_ref[...] * _dot(merged, wo_ref[...])


def _merge(x, o_f, o_b, g, o_conv, o_diff_p, o_diff_s, gates, mod4, gla_norm_g, wb, wo, l, blk):
    n = x.shape[0]
    npb = blk.n_pb

    def tok(width):
        return pl.BlockSpec((MM_BLOCK, width), lambda t: (t, 0))

    return pl.pallas_call(
        functools.partial(_merge_kernel, n_prompt_blocks=npb),
        grid=(blk.n_blocks,),
        in_specs=[
            tok(D_MODEL), tok(GLA_VW), tok(GLA_VW), tok(GLA_VW), tok(CONV_CH),
            pl.BlockSpec((MM_BLOCK, DIFF_VW), lambda t: (jnp.minimum(t, npb - 1), 0)),
            pl.BlockSpec((MM_BLOCK, DIFF_VW), lambda t: (jnp.maximum(t - npb, 0), 0)),
            tok(N_BRANCH * D_MODEL),
            pl.BlockSpec((None, None, 1, D_MODEL), lambda t: (l, blk.cond_row(t), 0, 2)),
            _resident((None, 1, GLA_DV), lambda t: (l, 0, 0)),
            _resident((None, GLA_VW + CONV_CH + DIFF_VW, D_MODEL), lambda t: (l, 0, 0)),
            _resident((None, D_MODEL, D_MODEL), lambda t: (l, 0, 0)),
        ],
        out_specs=tok(D_MODEL),
        out_shape=jax.ShapeDtypeStruct((n, D_MODEL), F32),
        compiler_params=_params(1),
        name=f"merge_out_{l}",
    )(x, o_f, o_b, g, o_conv, o_diff_p, o_diff_s, gates, mod4, gla_norm_g, wb, wo)


def _ffn_kernel(x_ref, sh2_ref, sc2_ref, g2_ref, n2g_ref, fng_ref, wfi_ref, wfo_ref, out_ref, acc_ref, *, final):
    x1 = x_ref[...]
    ms = jnp.mean(x1 * x1, axis=-1, keepdims=True)
    h2 = x1 * lax.rsqrt(ms + EPS) * n2g_ref[...]
    h2 = (h2 * (1.0 + sc2_ref[...]) + sh2_ref[...]).astype(BF16)
    for i, c0 in enumerate(range(0, FFN_HIDDEN, FFN_CHUNK)):
        a = _dot(h2, wfi_ref[:, c0:c0 + FFN_CHUNK])
        b = _dot(h2, wfi_ref[:, FFN_HIDDEN + c0:FFN_HIDDEN + c0 + FFN_CHUNK])
        u = (_silu(a) * b).astype(BF16)
        y = _dot(u, wfo_ref[c0:c0 + FFN_CHUNK, :])
        if i == 0:
            acc_ref[...] = y
        else:
            acc_ref[...] += y
    x2 = x_ref[...] + g2_ref[...] * acc_ref[...]
    if final:
        ms = jnp.mean(x2 * x2, axis=-1, keepdims=True)
        out_ref[...] = x2 * lax.rsqrt(ms + EPS) * fng_ref[...]
    else:
        out_ref[...] = x2


def _ffn(x, mod4, norm2_g, final_g, wfi, wfo, l, blk, final):
    n = x.shape[0]

    def mod(j):
        return pl.BlockSpec((None, None, 1, D_MODEL), lambda t: (l, blk.cond_row(t), 0, j))

    return pl.pallas_call(
        functools.partial(_ffn_kernel, final=final),
        grid=(blk.n_blocks,),
        in_specs=[
            pl.BlockSpec((FFN_BLOCK, D_MODEL), lambda t: (t, 0)),
            mod(3), mod(4), mod(5),
            _resident((None, 1, D_MODEL), lambda t: (l, 0, 0)),
            _resident((1, D_MODEL), lambda t: (0, 0)),
            _resident((None, D_MODEL, 2 * FFN_HIDDEN), lambda t: (l, 0, 0)),
            _resident((None, FFN_HIDDEN, D_MODEL), lambda t: (l, 0, 0)),
        ],
        out_specs=pl.BlockSpec((FFN_BLOCK, D_MODEL), lambda t: (t, 0)),
        out_shape=jax.ShapeDtypeStruct((n, D_MODEL), F32),
        scratch_shapes=[pltpu.VMEM((FFN_BLOCK, D_MODEL), F32)],
        compiler_params=_params(1),
        name=f"ffn_{l}",
    )(x, mod4, mod4, mod4, norm2_g, final_g, wfi, wfo)


def _rope_tables(n_tokens):
    rows = n_tokens // GRID_W
    row = jnp.repeat(jnp.arange(rows, dtype=F32), GRID_W)
    col = jnp.tile(jnp.arange(GRID_W, dtype=F32), rows)
    inv = ROPE_BASE ** (-jnp.arange(ROPE_FREQS, dtype=F32) / ROPE_FREQS)
    ang = jnp.stack([row[:, None] * inv, col[:, None] * inv], axis=1)
    cos, sin = jnp.cos(ang), jnp.sin(ang)
    cos_g = jnp.concatenate([cos, cos], axis=-1).reshape(n_tokens, 4 * ROPE_FREQS)
    sin_g = jnp.concatenate([-sin, sin], axis=-1).reshape(n_tokens, 4 * ROPE_FREQS)
    return jnp.tile(cos_g, (1, 2)), jnp.tile(sin_g, (1, 2))


def kernel(x_prompt, x_sample, cache_diff_k, cache_diff_v, state_gla, c, c_ctx, w_mod, b_mod, norm1_g, norm2_g,
           w_in, gla_w_a2, gla_b_a, gla_norm_g, conv_w, conv_b, conv_ln_g, conv_ln_b, diff_lambda, diff_subln_g,
           w_branch, w_out, w_ffn_in, w_ffn_out, final_norm_g):
    batch, seq, d = x_prompt.shape
    dec_batch, dec_seq, _ = x_sample.shape
    depth = w_mod.shape[0]
    assert d == D_MODEL and seq == TOKEN_BLOCK and dec_seq % TOKEN_BLOCK == 0 and dec_batch + 1 <= COND_ROWS
    blk = _Blocks(batch, dec_batch, dec_seq // TOKEN_BLOCK)
    n_p = batch * seq
    assert n_p % FFN_BLOCK == 0 and dec_seq % FFN_BLOCK == 0 and FFN_BLOCK % MM_BLOCK == 0
    blk_mm = _Blocks(n_p // MM_BLOCK, dec_batch, dec_seq // MM_BLOCK)
    blk_ffn = _Blocks(n_p // FFN_BLOCK, dec_batch, dec_seq // FFN_BLOCK)

    x = jnp.concatenate([x_prompt.reshape(n_p, d), x_sample.reshape(dec_batch * dec_seq, d)], axis=0)
    cond = jnp.concatenate([c_ctx[None, :], c, jnp.zeros((COND_ROWS - 1 - dec_batch, d), F32)], axis=0)
    assert w_in.shape[-1] == OFFS[10]
    w_gla = w_in[..., OFFS[0]:OFFS[4]].astype(BF16)
    w_rest = w_in[..., OFFS[5]:OFFS[10]].astype(BF16)
    w_lr = jnp.pad(w_in[..., OFFS[4]:OFFS[5]], ((0, 0), (0, 0), (0, LR_PAD - 2 * GLA_RANK))).astype(BF16)
    wa_p = jnp.zeros((depth, LR_PAD, 2 * GLA_QW), F32)
    wa_p = wa_p.at[:, 0:GLA_RANK, 0:GLA_QW].set(gla_w_a2[:, 0])
    wa_p = wa_p.at[:, GLA_RANK:2 * GLA_RANK, GLA_QW:].set(gla_w_a2[:, 1]).astype(BF16)
    ba_p = gla_b_a.reshape(depth, 1, 2 * GLA_QW)
    wb = w_branch.astype(BF16)
    wo = w_out.astype(BF16)
    wfi = w_ffn_in.astype(BF16)
    wfo = w_ffn_out.astype(BF16)
    vec = lambda a: a.reshape(depth, 1, a.shape[-1])
    rope_tabs = _rope_tables(dec_seq)
    ck = cache_diff_k.reshape(dec_batch, depth, -1, DIFF_QW)
    cv = cache_diff_v.reshape(dec_batch, depth, -1, DIFF_VW)
    hk = N_HEADS * GLA_DK
    s0_all = jnp.concatenate(
        [jnp.zeros((batch, depth, 2, hk, GLA_DV), F32), state_gla.reshape(dec_batch, depth, 2, hk, GLA_DV)], axis=0)

    mod4 = _modulation(cond, w_mod, b_mod).reshape(depth, COND_ROWS, 1, 6 * d)

    new_k, new_v, new_s = [], [], []
    for l in range(depth):
        q, k, v, g, la, cu, dq, dk, dv, gates = _in_projection(x, mod4, vec(norm1_g), w_gla, w_rest, w_lr, wa_p, ba_p, l, blk_mm)
        o_f, o_b, s_out = _gla(q, k, v, la, s0_all[:, l], blk)
        o_conv = _conv(cu, conv_w, vec(conv_b), vec(conv_ln_g), vec(conv_ln_b), l, blk)
        o_diff_p, o_diff_s = _diff_attention(dq, dk, dv, ck, cv, rope_tabs, diff_lambda, vec(diff_subln_g), l, blk)
        x = _merge(x, o_f, o_b, g, o_conv, o_diff_p, o_diff_s, gates, mod4, vec(gla_norm_g), wb, wo, l, blk_mm)
        x = _ffn(x, mod4, vec(norm2_g), final_norm_g.reshape(1, d), wfi, wfo, l, blk_ffn, final=(l == depth - 1))
        new_k.append(dk[:n_p].reshape(batch, seq, N_HEADS, 2, DIFF_DH))
        new_v.append(dv[:n_p].reshape(batch, seq, N_HEADS, DIFF_DV))
        new_s.append(s_out[:batch].reshape(batch, 2, N_HEADS, GLA_DK, GLA_DV))
    y_prompt = x[:n_p].reshape(batch, seq, d)
    y_sample = x[n_p:].reshape(dec_batch, dec_seq, d)
    return (y_prompt, y_sample, jnp.stack(new_k, axis=1), jnp.stack(new_v, axis=1), jnp.stack(new_s, axis=1))
```

```python
import functools
import math

import jax
import jax.numpy as jnp
from jax import lax
from jax.experimental import pallas as pl
from jax.experimental.pallas import tpu as pltpu

F32 = jnp.float32
BF16 = jnp.bfloat16

D_MODEL = 1024
N_HEADS = 4
GLA_DK = 64
GLA_DV = 128
GLA_RANK = 16
GLA_TEMP = 16.0
GLA_CHUNK = 64
CONV_CH = 512
CONV_WIDTH = 31
CONV_PAD = (CONV_WIDTH - 1) // 2
DIFF_DH = 64
DIFF_DV = 128
ROPE_FREQS = DIFF_DH // 4
ROPE_BASE = 10000.0
GRID_W = 64
FFN_HIDDEN = -(-8 * D_MODEL // (3 * 256)) * 256
EPS = 1e-6
N_BRANCH = 3

GLA_QW = N_HEADS * GLA_DK
GLA_VW = N_HEADS * GLA_DV
DIFF_QW = N_HEADS * 2 * DIFF_DH
DIFF_VW = N_HEADS * DIFF_DV
SPLITS = (GLA_QW, GLA_QW, GLA_VW, GLA_VW, 2 * GLA_RANK, 2 * CONV_CH, DIFF_QW, DIFF_QW, DIFF_VW, N_BRANCH * D_MODEL)
OFFS = tuple(sum(SPLITS[:i]) for i in range(len(SPLITS) + 1))

TOKEN_BLOCK = 256
MM_BLOCK = 512
FFN_BLOCK = 1024
LANES = 128
SUBLANES = 8
HALO = 16
CONV_ROWS = TOKEN_BLOCK + 2 * HALO - SUBLANES
COND_ROWS = 8
LR_PAD = LANES
FFN_CHUNK = 256
VMEM_LIMIT = 60 * 1024 * 1024

C_Q, C_K, C_V, C_G = 0, GLA_QW, 2 * GLA_QW, 2 * GLA_QW + GLA_VW
C_CU = C_G + GLA_VW
C_DQ = C_CU + 2 * CONV_CH
C_DK = C_DQ + DIFF_QW
C_DV = C_DK + DIFF_QW
C_GT = C_DV + DIFF_VW
C_LR = C_GT + N_BRANCH * D_MODEL


def _params(n_axes=1):
    return pltpu.CompilerParams(dimension_semantics=("arbitrary",) * n_axes, vmem_limit_bytes=VMEM_LIMIT)


def _resident(shape, index_map):
    return pl.BlockSpec(shape, index_map, pipeline_mode=pl.Buffered(1))


def _silu(x):
    return x * jax.nn.sigmoid(x)


def _dot(a, b):
    return jnp.dot(a, b, preferred_element_type=F32)


def _dot_nt(a, b):
    return lax.dot_general(a, b, (((1,), (1,)), ((), ())), preferred_element_type=F32)


def _dot_tn(a, b):
    return lax.dot_general(a, b, (((0,), (0,)), ((), ())), preferred_element_type=F32)


class _Blocks:
    def __init__(self, n_pb, n_ss, bps):
        self.n_pb, self.n_ss, self.bps = n_pb, n_ss, bps
        self.n_blocks = n_pb + n_ss * bps
        self.n_seq = n_pb + n_ss

    def is_prompt(self, t):
        return t < self.n_pb

    def cond_row(self, t):
        return jnp.where(t < self.n_pb, 0, 1 + (t - self.n_pb) // self.bps)

    def seq(self, t):
        return jnp.where(t < self.n_pb, t, self.n_pb + (t - self.n_pb) // self.bps)

    def pos(self, t):
        return jnp.where(t < self.n_pb, 0, (t - self.n_pb) % self.bps)

    def last_pos(self, t):
        return jnp.where(t < self.n_pb, 0, self.bps - 1)

    def rev(self, t):
        u = t - self.n_pb
        return jnp.where(t < self.n_pb, t, self.n_pb + (u // self.bps) * self.bps + (self.bps - 1 - u % self.bps))


def _mod_kernel(c_ref, w_ref, b_ref, o_ref):
    s = _silu(c_ref[...]).astype(BF16)
    o_ref[...] = _dot(s, w_ref[...].astype(BF16)) + b_ref[...]


def _modulation(cond, w_mod, b_mod):
    depth = w_mod.shape[0]
    tn = 1536
    return pl.pallas_call(
        _mod_kernel,
        grid=(depth, 6 * D_MODEL // tn),
        in_specs=[
            pl.BlockSpec((COND_ROWS, D_MODEL), lambda l, j: (0, 0)),
            pl.BlockSpec((None, D_MODEL, tn), lambda l, j: (l, 0, j)),
            pl.BlockSpec((None, 1, tn), lambda l, j: (l, 0, j)),
        ],
        out_specs=pl.BlockSpec((None, COND_ROWS, tn), lambda l, j: (l, 0, j)),
        out_shape=jax.ShapeDtypeStruct((depth, COND_ROWS, 6 * D_MODEL), F32),
        compiler_params=_params(2),
        name="modulation",
    )(cond, w_mod, b_mod.reshape(depth, 1, 6 * D_MODEL))


def _in_kernel(x_ref, sh_ref, sc_ref, ng_ref, wg_ref, wr_ref, wl_ref, wa_ref, ba_ref,
               q_ref, k_ref, v_ref, g_ref, la_ref, cu_ref, dq_ref, dk_ref, dv_ref, gt_ref):
    x = x_ref[...]
    ms = jnp.mean(x * x, axis=-1, keepdims=True)
    h = x * lax.rsqrt(ms + EPS) * ng_ref[...]
    h = (h * (1.0 + sc_ref[...]) + sh_ref[...]).astype(BF16)

    def proj(w_ref, c0, width, out_ref, fn=None):
        step = 512
        for a in range(0, width, step):
            b = min(a + step, width)
            r = _dot(h, w_ref[:, c0 + a:c0 + b])
            out_ref[:, a:b] = r if fn is None else fn(r)

    proj(wg_ref, C_Q, GLA_QW, q_ref)
    proj(wg_ref, C_K, GLA_QW, k_ref)
    proj(wg_ref, C_V, GLA_VW, v_ref)
    proj(wg_ref, C_G, GLA_VW, g_ref)
    proj(wr_ref, 0, 2 * CONV_CH, cu_ref)
    proj(wr_ref, C_DQ - C_CU, DIFF_QW, dq_ref)
    proj(wr_ref, C_DK - C_CU, DIFF_QW, dk_ref)
    proj(wr_ref, C_DV - C_CU, DIFF_VW, dv_ref)
    proj(wr_ref, C_GT - C_CU, N_BRANCH * D_MODEL, gt_ref, lambda r: jax.nn.sigmoid(r).astype(BF16))
    lr = _dot(h, wl_ref[...]).astype(BF16)
    xa = _dot(lr, wa_ref[...]) + ba_ref[...]
    la_ref[...] = (jnp.minimum(xa, 0.0) - jnp.log1p(jnp.exp(-jnp.abs(xa)))) * (1.0 / GLA_TEMP)


def _in_projection(x, mod4, norm_g, w_gla, w_rest, w_lr, wa_p, ba_p, l, blk):
    n = x.shape[0]
    row = blk.cond_row

    def tok(width):
        return pl.BlockSpec((MM_BLOCK, width), lambda t: (t, 0))

    widths = (GLA_QW, GLA_QW, GLA_VW, GLA_VW, 2 * GLA_QW, 2 * CONV_CH, DIFF_QW, DIFF_QW, DIFF_VW, N_BRANCH * D_MODEL)
    return pl.pallas_call(
        _in_kernel,
        grid=(blk.n_blocks,),
        in_specs=[
            tok(D_MODEL),
            pl.BlockSpec((None, None, 1, D_MODEL), lambda t: (l, row(t), 0, 0)),
            pl.BlockSpec((None, None, 1, D_MODEL), lambda t: (l, row(t), 0, 1)),
            _resident((None, 1, D_MODEL), lambda t: (l, 0, 0)),
            _resident((None, D_MODEL, C_CU), lambda t: (l, 0, 0)),
            _resident((None, D_MODEL, C_LR - C_CU), lambda t: (l, 0, 0)),
            _resident((None, D_MODEL, LR_PAD), lambda t: (l, 0, 0)),
            _resident((None, LR_PAD, 2 * GLA_QW), lambda t: (l, 0, 0)),
            _resident((None, 1, 2 * GLA_QW), lambda t: (l, 0, 0)),
        ],
        out_specs=[tok(w) for w in widths],
        out_shape=[jax.ShapeDtypeStruct((n, w), F32) for w in widths[:-1]]
        + [jax.ShapeDtypeStruct((n, widths[-1]), BF16)],
        compiler_params=_params(1),
        name=f"in_projection_{l}",
    )(x, mod4, mod4, norm_g, w_gla, w_rest, w_lr, wa_p, ba_p)


def _gla_prepare(q_ref, k_ref, v_ref, la_ref, tri, causal, lo, order, mid, last):
    la = la_ref[...]
    hi = la.astype(BF16)
    r1 = la - hi.astype(F32)
    md = r1.astype(BF16)
    lw = (r1 - md.astype(F32)).astype(BF16)
    b = _dot(tri, hi) + _dot(tri, md) + _dot(tri, lw)
    q = q_ref[...] * (GLA_DK ** -0.5)
    k = k_ref[...]
    c = GLA_CHUNK
    pair = 2 * GLA_DK
    chunks = []
    for n in order:
        sl = slice(n * c, (n + 1) * c)
        bn = b[sl]
        bmid = bn[mid:mid + 1]
        blast = bn[last:last + 1]
        qn, kn = q[sl], k[sl]
        vn = v_ref[sl, :].astype(BF16)
        qe = qn * jnp.exp(bn)
        qt = qn * jnp.exp(bn - bmid)
        kt = (kn * jnp.exp(bmid - bn)).astype(BF16)
        ku = (kn * jnp.exp(blast - bn)).astype(BF16)
        decay = jnp.exp(jnp.broadcast_to(blast, (LANES, N_HEADS * GLA_DK)).T)
        qes, atts, incs = [], [], []
        for p in range(N_HEADS // 2):
            cs = slice(p * pair, (p + 1) * pair)
            qt_p, qe_p = qt[:, cs], qe[:, cs]
            lhs = jnp.concatenate([jnp.where(lo, qt_p, 0.0), jnp.where(lo, 0.0, qt_p)], axis=0).astype(BF16)
            att = _dot_nt(lhs, kt[:, cs])
            atts += [jnp.where(causal, att[0:c], 0.0).astype(BF16), jnp.where(causal, att[c:2 * c], 0.0).astype(BF16)]
            qes += [jnp.where(lo, qe_p, 0.0).astype(BF16), jnp.where(lo, 0.0, qe_p).astype(BF16)]
            uf = _dot_tn(ku[:, cs], vn[:, (2 * p) * GLA_DV:(2 * p + 2) * GLA_DV])
            incs += [uf[0:GLA_DK, 0:GLA_DV], uf[GLA_DK:2 * GLA_DK, GLA_DV:2 * GLA_DV]]
        chunks.append((sl, qes, atts, vn, decay, jnp.concatenate(incs, axis=0)))
    return chunks


def _gla_scan(chunks, s_scr, o_ref):
    pair = 2 * GLA_DK
    s = s_scr[...]
    for sl, qes, atts, vn, decay, inc in chunks:
        s_b16 = s.astype(BF16)
        for h in range(N_HEADS):
            s_p = s_b16[(h // 2) * pair:(h // 2 + 1) * pair]
            cs = slice(h * GLA_DV, (h + 1) * GLA_DV)
            o_ref[sl, cs] = _dot(qes[h], s_p) + _dot(atts[h], vn[:, cs])
        s = s * decay + inc
    s_scr[...] = s


def _gla_kernel(qf_ref, kf_ref, vf_ref, laf_ref, qb_ref, kb_ref, vb_ref, lab_ref, s0_ref,
                of_ref, ob_ref, sout_ref, sf_scr, sb_scr, *, blk):
    t = pl.program_id(0)
    pos = blk.pos(t)

    @pl.when(pos == 0)
    def _():
        sf_scr[...] = s0_ref[0]
        sb_scr[...] = s0_ref[1]

    tb, c = TOKEN_BLOCK, GLA_CHUNK
    shift = int(math.log2(c))
    row = lax.broadcasted_iota(jnp.int32, (tb, tb), 0)
    col = lax.broadcasted_iota(jnp.int32, (tb, tb), 1)
    same = jnp.right_shift(row, shift) == jnp.right_shift(col, shift)
    tri_f = jnp.where(same, jnp.where(col <= row, 1.0, 0.0), 0.0).astype(BF16)
    tri_b = jnp.where(same, jnp.where(col >= row, 1.0, 0.0), 0.0).astype(BF16)
    r64 = lax.broadcasted_iota(jnp.int32, (c, c), 0)
    c64 = lax.broadcasted_iota(jnp.int32, (c, c), 1)
    lo = lax.broadcasted_iota(jnp.int32, (c, 2 * GLA_DK), 1) < GLA_DK
    n_chunks = tb // c
    fwd = _gla_prepare(qf_ref, kf_ref, vf_ref, laf_ref, tri_f, c64 <= r64, lo,
                       tuple(range(n_chunks)), c // 2 - 1, c - 1)
    bwd = _gla_prepare(qb_ref, kb_ref, vb_ref, lab_ref, tri_b, c64 >= r64, lo,
                       tuple(reversed(range(n_chunks))), c // 2, 0)
    _gla_scan(fwd, sf_scr, of_ref)
    _gla_scan(bwd, sb_scr, ob_ref)

    @pl.when(pos == blk.last_pos(t))
    def _():
        sout_ref[0] = sf_scr[...]
        sout_ref[1] = sb_scr[...]


def _gla(q, k, v, la, s0, blk):
    n = q.shape[0]
    rev = blk.rev
    hk, hv = N_HEADS * GLA_DK, N_HEADS * GLA_DV

    def fwd(width, cb=0):
        return pl.BlockSpec((TOKEN_BLOCK, width), lambda t: (t, cb))

    def bwd(width, cb=0):
        return pl.BlockSpec((TOKEN_BLOCK, width), lambda t: (rev(t), cb))

    state_spec = pl.BlockSpec((None, 2, hk, GLA_DV), lambda t: (blk.seq(t), 0, 0, 0))
    return pl.pallas_call(
        functools.partial(_gla_kernel, blk=blk),
        grid=(blk.n_blocks,),
        in_specs=[fwd(hk), fwd(hk), fwd(hv), fwd(hk, 0), bwd(hk), bwd(hk), bwd(hv), bwd(hk, 1), state_spec],
        out_specs=[fwd(hv), bwd(hv), state_spec],
        out_shape=[jax.ShapeDtypeStruct((n, hv), F32), jax.ShapeDtypeStruct((n, hv), F32),
                   jax.ShapeDtypeStruct((blk.n_seq, 2, hk, GLA_DV), F32)],
        scratch_shapes=[pltpu.VMEM((hk, GLA_DV), F32), pltpu.VMEM((hk, GLA_DV), F32)],
        compiler_params=_params(1),
        name="gla_scan",
    )(q, k, v, la, q, k, v, la, s0)


def _conv_kernel(prev_ref, cur_ref, next_ref, w_ref, b_ref, lg_ref, lb_ref, o_ref, y_scr, ys_scr, *, blk):
    t = pl.program_id(0)
    pos = blk.pos(t)
    ch = CONV_CH

    def glu(u):
        return u[:, :ch] * jax.nn.sigmoid(u[:, ch:])

    y_scr[0:HALO, :] = jnp.where(pos == 0, 0.0, glu(prev_ref[...]))
    y_scr[HALO:HALO + TOKEN_BLOCK, :] = glu(cur_ref[...])
    y_scr[HALO + TOKEN_BLOCK:, :] = jnp.where(pos == blk.last_pos(t), 0.0, glu(next_ref[...]))

    for s in range(1, SUBLANES):
        ys_scr[s - 1] = y_scr[s:s + CONV_ROWS, :]

    rows = 32
    first = HALO - CONV_PAD
    for r0 in range(0, TOKEN_BLOCK, rows):
        acc = jnp.zeros((rows, ch), F32)
        for w in range(CONV_WIDTH):
            s, a = (first + w) % SUBLANES, r0 + (first + w) // SUBLANES * SUBLANES
            tap = y_scr[a:a + rows, :] if s == 0 else ys_scr[s - 1, a:a + rows, :]
            acc = acc + tap * w_ref[w:w + 1, :]
        y = acc + b_ref[...]
        mu = jnp.mean(y, axis=-1, keepdims=True)
        d = y - mu
        var = jnp.mean(d * d, axis=-1, keepdims=True)
        z = d * lax.rsqrt(var + EPS) * lg_ref[...] + lb_ref[...]
        o_ref[r0:r0 + rows, :] = _silu(z)


def _conv(cu, conv_w, conv_b, ln_g, ln_b, l, blk):
    n = cu.shape[0]
    hb = TOKEN_BLOCK // HALO
    n_halo = n // HALO
    vec = lambda: _resident((None, 1, CONV_CH), lambda t: (l, 0, 0))
    return pl.pallas_call(
        functools.partial(_conv_kernel, blk=blk),
        grid=(blk.n_blocks,),
        in_specs=[
            pl.BlockSpec((HALO, 2 * CONV_CH), lambda t: (jnp.maximum(t * hb - 1, 0), 0)),
            pl.BlockSpec((TOKEN_BLOCK, 2 * CONV_CH), lambda t: (t, 0)),
            pl.BlockSpec((HALO, 2 * CONV_CH), lambda t: (jnp.minimum((t + 1) * hb, n_halo - 1), 0)),
            _resident((None, CONV_WIDTH, CONV_CH), lambda t: (l, 0, 0)),
            vec(), vec(), vec(),
        ],
        out_specs=pl.BlockSpec((TOKEN_BLOCK, CONV_CH), lambda t: (t, 0)),
        out_shape=jax.ShapeDtypeStruct((n, CONV_CH), F32),
        scratch_shapes=[pltpu.VMEM((TOKEN_BLOCK + 2 * HALO, CONV_CH), F32),
                        pltpu.VMEM((SUBLANES - 1, CONV_ROWS, CONV_CH), F32)],
        compiler_params=_params(1),
        name=f"conv_module_{l}",
    )(cu, cu, cu, conv_w, conv_b, ln_g, ln_b)


def _rope(x, cos, sin_signed, first_half):
    partner = jnp.where(first_half, pltpu.roll(x, LANES - ROPE_FREQS, 1), pltpu.roll(x, ROPE_FREQS, 1))
    return x * cos + partner * sin_signed


def _diff_kernel(*refs, rope, lam_init, n_self, n_cache, tq):
    if rope:
        (q_ref, k_ref, v_ref, ck_ref, cv_ref, cosq_ref, sinq_ref, cosk_ref, sink_ref, lam_ref, g_ref,
         o_ref, k_scr, v_scr) = refs
    else:
        q_ref, k_ref, v_ref, lam_ref, g_ref, o_ref, k_scr, v_scr = refs
    slab = 2 * DIFF_DH
    lane = lax.broadcasted_iota(jnp.int32, (1, slab), 1)
    first_half = (lane % (2 * ROPE_FREQS)) < ROPE_FREQS
    lo = lane < DIFF_DH

    @pl.when(pl.program_id(1) == 0)
    def _():
        rows = 256
        for r0 in range(0, n_self, rows):
            rs = slice(r0, r0 + rows)
            for h in range(N_HEADS):
                cs = slice(h * slab, (h + 1) * slab)
                kx = k_ref[rs, cs]
                if rope:
                    kx = _rope(kx, cosk_ref[rs, :], sink_ref[rs, :], first_half)
                k_scr[rs, cs] = kx.astype(BF16)
            for h in range(N_HEADS):
                v_scr[rs, (2 * h) * DIFF_DV:(2 * h + 1) * DIFF_DV] = v_ref[rs, h * DIFF_DV:(h + 1) * DIFF_DV].astype(BF16)
                v_scr[rs, (2 * h + 1) * DIFF_DV:(2 * h + 2) * DIFF_DV] = jnp.ones((rows, DIFF_DV), BF16)
        if n_cache:
            cs_rows = slice(n_self, n_self + n_cache)
            k_scr[cs_rows, :] = ck_ref[...].astype(BF16)
            for h in range(N_HEADS):
                v_scr[cs_rows, (2 * h) * DIFF_DV:(2 * h + 1) * DIFF_DV] = cv_ref[:, h * DIFF_DV:(h + 1) * DIFF_DV].astype(BF16)
                v_scr[cs_rows, (2 * h + 1) * DIFF_DV:(2 * h + 2) * DIFF_DV] = jnp.ones((n_cache, DIFF_DV), BF16)

    lv = lam_ref[...]
    lam = (jnp.exp(jnp.sum(lv[0:1] * lv[1:2], axis=-1, keepdims=True))
           - jnp.exp(jnp.sum(lv[2:3] * lv[3:4], axis=-1, keepdims=True)) + lam_init)
    scale = DIFF_DH ** -0.5
    for h in range(N_HEADS):
        cs = slice(h * slab, (h + 1) * slab)
        qh = q_ref[:, cs]
        if rope:
            qh = _rope(qh, cosq_ref[...], sinq_ref[...], first_half)
        qh = qh * scale
        lhs = jnp.concatenate([jnp.where(lo, qh, 0.0), jnp.where(lo, 0.0, qh)], axis=0).astype(BF16)
        s = _dot_nt(lhs, k_scr[:, cs])
        p = jnp.exp(s - jnp.max(s, axis=-1, keepdims=True)).astype(BF16)
        pv = _dot(p, v_scr[:, (2 * h) * DIFF_DV:(2 * h + 2) * DIFF_DV])
        o = (pv[0:tq, 0:DIFF_DV] * (1.0 / pv[0:tq, DIFF_DV:])
             - pv[tq:, 0:DIFF_DV] * (lam / pv[tq:, DIFF_DV:]))
        ms = jnp.mean(o * o, axis=-1, keepdims=True)
        o_ref[:, h * DIFF_DV:(h + 1) * DIFF_DV] = o * lax.rsqrt(ms + EPS) * g_ref[...] * (1.0 - lam_init)


def _diff_attention(dq, dk, dv, cache_k, cache_v, rope_tabs, diff_lambda, subln_g, l, blk):
    n = dq.shape[0]
    lam_init = 0.8 - 0.6 * math.exp(-0.3 * l)
    n_p = blk.n_pb * TOKEN_BLOCK
    ls = blk.bps * TOKEN_BLOCK
    lam_spec = lambda: pl.BlockSpec((None, 4, DIFF_DH), lambda s, i: (l, 0, 0))
    g_spec = lambda: pl.BlockSpec((None, 1, DIFF_DV), lambda s, i: (l, 0, 0))

    o_prompt = pl.pallas_call(
        functools.partial(_diff_kernel, rope=False, lam_init=lam_init, n_self=TOKEN_BLOCK, n_cache=0, tq=TOKEN_BLOCK),
        grid=(blk.n_pb, 1),
        in_specs=[pl.BlockSpec((TOKEN_BLOCK, DIFF_QW), lambda s, i: (s, 0)),
                  pl.BlockSpec((TOKEN_BLOCK, DIFF_QW), lambda s, i: (s, 0)),
                  pl.BlockSpec((TOKEN_BLOCK, DIFF_VW), lambda s, i: (s, 0)),
                  lam_spec(), g_spec()],
        out_specs=pl.BlockSpec((TOKEN_BLOCK, DIFF_VW), lambda s, i: (s, 0)),
        out_shape=jax.ShapeDtypeStruct((n_p, DIFF_VW), F32),
        scratch_shapes=[pltpu.VMEM((TOKEN_BLOCK, DIFF_QW), BF16), pltpu.VMEM((TOKEN_BLOCK, 2 * DIFF_VW), BF16)],
        compiler_params=_params(2),
        name=f"diff_attn_prompt_{l}",
    )(dq, dk, dv, diff_lambda, subln_g)

    tq = TOKEN_BLOCK
    n_cache = cache_k.shape[2]
    cos_t, sin_t = rope_tabs
    seq0 = n_p // ls
    assert seq0 * ls == n_p
    q0 = n_p // tq
    o_all = pl.pallas_call(
        functools.partial(_diff_kernel, rope=True, lam_init=lam_init, n_self=ls, n_cache=n_cache, tq=tq),
        grid=(blk.n_ss, ls // tq),
        in_specs=[pl.BlockSpec((tq, DIFF_QW), lambda s, i: (q0 + s * (ls // tq) + i, 0)),
                  pl.BlockSpec((ls, DIFF_QW), lambda s, i: (seq0 + s, 0)),
                  pl.BlockSpec((ls, DIFF_VW), lambda s, i: (seq0 + s, 0)),
                  pl.BlockSpec((None, None, n_cache, DIFF_QW), lambda s, i: (s, l, 0, 0)),
                  pl.BlockSpec((None, None, n_cache, DIFF_VW), lambda s, i: (s, l, 0, 0)),
                  pl.BlockSpec((tq, LANES), lambda s, i: (i, 0)),
                  pl.BlockSpec((tq, LANES), lambda s, i: (i, 0)),
                  _resident((ls, LANES), lambda s, i: (0, 0)),
                  _resident((ls, LANES), lambda s, i: (0, 0)),
                  lam_spec(), g_spec()],
        out_specs=pl.BlockSpec((tq, DIFF_VW), lambda s, i: (s * (ls // tq) + i, 0)),
        out_shape=jax.ShapeDtypeStruct((n - n_p, DIFF_VW), F32),
        scratch_shapes=[pltpu.VMEM((ls + n_cache, DIFF_QW), BF16), pltpu.VMEM((ls + n_cache, 2 * DIFF_VW), BF16)],
        compiler_params=_params(2),
        name=f"diff_attn_latent_{l}",
    )(dq, dk, dv, cache_k, cache_v, cos_t, sin_t, cos_t, sin_t, diff_lambda, subln_g)
    return o_prompt, o_all


def _merge_kernel(x_ref, of_ref, ob_ref, g_ref, oc_ref, odp_ref, ods_ref, gt_ref, g1_ref, gng_ref, wb_ref, wo_ref,
                  out_ref, *, n_prompt_blocks):
    o = of_ref[...] + ob_ref[...]
    parts = []
    for h in range(N_HEADS):
        cs = slice(h * GLA_DV, (h + 1) * GLA_DV)
        oh = o[:, cs]
        ms = jnp.mean(oh * oh, axis=-1, keepdims=True)
        parts.append((oh * lax.rsqrt(ms + EPS) * gng_ref[...] * _silu(g_ref[:, cs])).astype(BF16))
    o_gla = jnp.concatenate(parts, axis=1)
    o_conv = oc_ref[...].astype(BF16)
    o_diff = jnp.where(pl.program_id(0) < n_prompt_blocks, odp_ref[...], ods_ref[...]).astype(BF16)
    b1 = GLA_VW
    b2 = b1 + CONV_CH
    b3 = b2 + DIFF_VW
    half = D_MODEL // 2
    merged = []
    for c0 in range(0, D_MODEL, half):
        cs = slice(c0, c0 + half)
        m = gt_ref[:, c0:c0 + half] * _dot(o_gla, wb_ref[0:b1, cs])
        m = m + gt_ref[:, D_MODEL + c0:D_MODEL + c0 + half] * _dot(o_conv, wb_ref[b1:b2, cs])
        m = m + gt_ref[:, 2 * D_MODEL + c0:2 * D_MODEL + c0 + half] * _dot(o_diff, wb_ref[b2:b3, cs])
        merged.append(m.astype(BF16))
    merged = jnp.concatenate(merged, axis=1)
    out_ref[...] = x_ref[...] + g1_ref[...] * _dot(merged, wo_ref[...])


def _merge(x, o_f, o_b, g, o_conv, o_diff_p, o_diff_s, gates, mod4, gla_norm_g, wb, wo, l, blk):
    n = x.shape[0]
    npb = blk.n_pb

    def tok(width):
        return pl.BlockSpec((MM_BLOCK, width), lambda t: (t, 0))

    return pl.pallas_call(
        functools.partial(_merge_kernel, n_prompt_blocks=npb),
        grid=(blk.n_blocks,),
        in_specs=[
            tok(D_MODEL), tok(GLA_VW), tok(GLA_VW), tok(GLA_VW), tok(CONV_CH),
            pl.BlockSpec((MM_BLOCK, DIFF_VW), lambda t: (jnp.minimum(t, npb - 1), 0)),
            pl.BlockSpec((MM_BLOCK, DIFF_VW), lambda t: (jnp.maximum(t - npb, 0), 0)),
            tok(N_BRANCH * D_MODEL),
            pl.BlockSpec((None, None, 1, D_MODEL), lambda t: (l, blk.cond_row(t), 0, 2)),
            _resident((None, 1, GLA_DV), lambda t: (l, 0, 0)),
            _resident((None, GLA_VW + CONV_CH + DIFF_VW, D_MODEL), lambda t: (l, 0, 0)),
            _resident((None, D_MODEL, D_MODEL), lambda t: (l, 0, 0)),
        ],
        out_specs=tok(D_MODEL),
        out_shape=jax.ShapeDtypeStruct((n, D_MODEL), F32),
        compiler_params=_params(1),
        name=f"merge_out_{l}",
    )(x, o_f, o_b, g, o_conv, o_diff_p, o_diff_s, gates, mod4, gla_norm_g, wb, wo)


def _ffn_kernel(x_ref, sh2_ref, sc2_ref, g2_ref, n2g_ref, fng_ref, wfi_ref, wfo_ref, out_ref, acc_ref, *, final):
    x1 = x_ref[...]
    ms = jnp.mean(x1 * x1, axis=-1, keepdims=True)
    h2 = x1 * lax.rsqrt(ms + EPS) * n2g_ref[...]
    h2 = (h2 * (1.0 + sc2_ref[...]) + sh2_ref[...]).astype(BF16)
    for i, c0 in enumerate(range(0, FFN_HIDDEN, FFN_CHUNK)):
        a = _dot(h2, wfi_ref[:, c0:c0 + FFN_CHUNK])
        b = _dot(h2, wfi_ref[:, FFN_HIDDEN + c0:FFN_HIDDEN + c0 + FFN_CHUNK])
        u = (_silu(a) * b).astype(BF16)
        y = _dot(u, wfo_ref[c0:c0 + FFN_CHUNK, :])
        if i == 0:
            acc_ref[...] = y
        else:
            acc_ref[...] += y
    x2 = x_ref[...] + g2_ref[...] * acc_ref[...]
    if final:
        ms = jnp.mean(x2 * x2, axis=-1, keepdims=True)
        out_ref[...] = x2 * lax.rsqrt(ms + EPS) * fng_ref[...]
    else:
        out_ref[...] = x2


def _ffn(x, mod4, norm2_g, final_g, wfi, wfo, l, blk, final):
    n = x.shape[0]

    def mod(j):
        return pl.BlockSpec((None, None, 1, D_MODEL), lambda t: (l, blk.cond_row(t), 0, j))

    return pl.pallas_call(
        functools.partial(_ffn_kernel, final=final),
        grid=(blk.n_blocks,),
        in_specs=[
            pl.BlockSpec((FFN_BLOCK, D_MODEL), lambda t: (t, 0)),
            mod(3), mod(4), mod(5),
            _resident((None, 1, D_MODEL), lambda t: (l, 0, 0)),
            _resident((1, D_MODEL), lambda t: (0, 0)),
            _resident((None, D_MODEL, 2 * FFN_HIDDEN), lambda t: (l, 0, 0)),
            _resident((None, FFN_HIDDEN, D_MODEL), lambda t: (l, 0, 0)),
        ],
        out_specs=pl.BlockSpec((FFN_BLOCK, D_MODEL), lambda t: (t, 0)),
        out_shape=jax.ShapeDtypeStruct((n, D_MODEL), F32),
        scratch_shapes=[pltpu.VMEM((FFN_BLOCK, D_MODEL), F32)],
        compiler_params=_params(1),
        name=f"ffn_{l}",
    )(x, mod4, mod4, mod4, norm2_g, final_g, wfi, wfo)


def _rope_tables(n_tokens):
    rows = n_tokens // GRID_W
    row = jnp.repeat(jnp.arange(rows, dtype=F32), GRID_W)
    col = jnp.tile(jnp.arange(GRID_W, dtype=F32), rows)
    inv = ROPE_BASE ** (-jnp.arange(ROPE_FREQS, dtype=F32) / ROPE_FREQS)
    ang = jnp.stack([row[:, None] * inv, col[:, None] * inv], axis=1)
    cos, sin = jnp.cos(ang), jnp.sin(ang)
    cos_g = jnp.concatenate([cos, cos], axis=-1).reshape(n_tokens, 4 * ROPE_FREQS)
    sin_g = jnp.concatenate([-sin, sin], axis=-1).reshape(n_tokens, 4 * ROPE_FREQS)
    return jnp.tile(cos_g, (1, 2)), jnp.tile(sin_g, (1, 2))


def kernel(x_prompt, x_sample, cache_diff_k, cache_diff_v, state_gla, c, c_ctx, w_mod, b_mod, norm1_g, norm2_g,
           w_in, gla_w_a2, gla_b_a, gla_norm_g, conv_w, conv_b, conv_ln_g, conv_ln_b, diff_lambda, diff_subln_g,
           w_branch, w_out, w_ffn_in, w_ffn_out, final_norm_g):
    batch, seq, d = x_prompt.shape
    dec_batch, dec_seq, _ = x_sample.shape
    depth = w_mod.shape[0]
    assert d == D_MODEL and seq == TOKEN_BLOCK and dec_seq % TOKEN_BLOCK == 0 and dec_batch + 1 <= COND_ROWS
    blk = _Blocks(batch, dec_batch, dec_seq // TOKEN_BLOCK)
    n_p = batch * seq
    assert n_p % FFN_BLOCK == 0 and dec_seq % FFN_BLOCK == 0 and FFN_BLOCK % MM_BLOCK == 0
    blk_mm = _Blocks(n_p // MM_BLOCK, dec_batch, dec_seq // MM_BLOCK)
    blk_ffn = _Blocks(n_p // FFN_BLOCK, dec_batch, dec_seq // FFN_BLOCK)

    x = jnp.concatenate([x_prompt.reshape(n_p, d), x_sample.reshape(dec_batch * dec_seq, d)], axis=0)
    cond = jnp.concatenate([c_ctx[None, :], c, jnp.zeros((COND_ROWS - 1 - dec_batch, d), F32)], axis=0)
    assert w_in.shape[-1] == OFFS[10]
    w_gla = w_in[..., OFFS[0]:OFFS[4]].astype(BF16)
    w_rest = w_in[..., OFFS[5]:OFFS[10]].astype(BF16)
    w_lr = jnp.pad(w_in[..., OFFS[4]:OFFS[5]], ((0, 0), (0, 0), (0, LR_PAD - 2 * GLA_RANK))).astype(BF16)
    wa_p = jnp.zeros((depth, LR_PAD, 2 * GLA_QW), F32)
    wa_p = wa_p.at[:, 0:GLA_RANK, 0:GLA_QW].set(gla_w_a2[:, 0])
    wa_p = wa_p.at[:, GLA_RANK:2 * GLA_RANK, GLA_QW:].set(gla_w_a2[:, 1]).astype(BF16)
    ba_p = gla_b_a.reshape(depth, 1, 2 * GLA_QW)
    wb = w_branch.astype(BF16)
    wo = w_out.astype(BF16)
    wfi = w_ffn_in.astype(BF16)
    wfo = w_ffn_out.astype(BF16)
    vec = lambda a: a.reshape(depth, 1, a.shape[-1])
    rope_tabs = _rope_tables(dec_seq)
    ck = cache_diff_k.reshape(dec_batch, depth, -1, DIFF_QW)
    cv = cache_diff_v.reshape(dec_batch, depth, -1, DIFF_VW)
    hk = N_HEADS * GLA_DK
    s0_all = jnp.concatenate(
        [jnp.zeros((batch, depth, 2, hk, GLA_DV), F32), state_gla.reshape(dec_batch, depth, 2, hk, GLA_DV)], axis=0)

    mod4 = _modulation(cond, w_mod, b_mod).reshape(depth, COND_ROWS, 1, 6 * d)

    new_k, new_v, new_s = [], [], []
    for l in range(depth):
        q, k, v, g, la, cu, dq, dk, dv, gates = _in_projection(x, mod4, vec(norm1_g), w_gla, w_rest, w_lr, wa_p, ba_p, l, blk_mm)
        o_f, o_b, s_out = _gla(q, k, v, la, s0_all[:, l], blk)
        o_conv = _conv(cu, conv_w, vec(conv_b), vec(conv_ln_g), vec(conv_ln_b), l, blk)
        o_diff_p, o_diff_s = _diff_attention(dq, dk, dv, ck, cv, rope_tabs, diff_lambda, vec(diff_subln_g), l, blk)
        x = _merge(x, o_f, o_b, g, o_conv, o_diff_p, o_diff_s, gates, mod4, vec(gla_norm_g), wb, wo, l, blk_mm)
        x = _ffn(x, mod4, vec(norm2_g), final_norm_g.reshape(1, d), wfi, wfo, l, blk_ffn, final=(l == depth - 1))
        new_k.append(dk[:n_p].reshape(batch, seq, N_HEADS, 2, DIFF_DH))
        new_v.append(dv[:n_p].reshape(batch, seq, N_HEADS, DIFF_DV))
        new_s.append(s_out[:batch].reshape(batch, 2, N_HEADS, GLA_DK, GLA_DV))
    y_prompt = x[:n_p].reshape(batch, seq, d)
    y_sample = x[n_p:].reshape(dec_batch, dec_seq, d)
    return (y_prompt, y_sample, jnp.stack(new_k, axis=1), jnp.stack(new_v, axis=1), jnp.stack(new_s, axis=1))
```

```python
import functools
import math

import jax
import jax.numpy as jnp
from jax import lax
from jax.experimental import pallas as pl
from jax.experimental.pallas import tpu as pltpu

F32 = jnp.float32
BF16 = jnp.bfloat16

D_MODEL = 1024
N_HEADS = 4
GLA_DK = 64
GLA_DV = 128
GLA_RANK = 16
GLA_TEMP = 16.0
GLA_CHUNK = 64
CONV_CH = 512
CONV_WIDTH = 31
CONV_PAD = (CONV_WIDTH - 1) // 2
DIFF_DH = 64
DIFF_DV = 128
ROPE_FREQS = DIFF_DH // 4
ROPE_BASE = 10000.0
GRID_W = 64
FFN_HIDDEN = -(-8 * D_MODEL // (3 * 256)) * 256
EPS = 1e-6
N_BRANCH = 3

GLA_QW = N_HEADS * GLA_DK
GLA_VW = N_HEADS * GLA_DV
DIFF_QW = N_HEADS * 2 * DIFF_DH
DIFF_VW = N_HEADS * DIFF_DV
SPLITS = (GLA_QW, GLA_QW, GLA_VW, GLA_VW, 2 * GLA_RANK, 2 * CONV_CH, DIFF_QW, DIFF_QW, DIFF_VW, N_BRANCH * D_MODEL)
OFFS = tuple(sum(SPLITS[:i]) for i in range(len(SPLITS) + 1))

TOKEN_BLOCK = 256
MM_BLOCK = 512
FFN_BLOCK = 1024
LANES = 128
SUBLANES = 8
HALO = 16
CONV_ROWS = TOKEN_BLOCK + 2 * HALO - SUBLANES
COND_ROWS = 8
LR_PAD = LANES
FFN_CHUNK = 256
VMEM_LIMIT = 60 * 1024 * 1024

C_Q, C_K, C_V, C_G = 0, GLA_QW, 2 * GLA_QW, 2 * GLA_QW + GLA_VW
C_CU = C_G + GLA_VW
C_DQ = C_CU + 2 * CONV_CH
C_DK = C_DQ + DIFF_QW
C_DV = C_DK + DIFF_QW
C_GT = C_DV + DIFF_VW
C_LR = C_GT + N_BRANCH * D_MODEL


def _params(n_axes=1):
    return pltpu.CompilerParams(dimension_semantics=("arbitrary",) * n_axes, vmem_limit_bytes=VMEM_LIMIT)


def _resident(shape, index_map):
    return pl.BlockSpec(shape, index_map, pipeline_mode=pl.Buffered(1))


def _silu(x):
    return x * jax.nn.sigmoid(x)


def _dot(a, b):
    return jnp.dot(a, b, preferred_element_type=F32)


def _dot_nt(a, b):
    return lax.dot_general(a, b, (((1,), (1,)), ((), ())), preferred_element_type=F32)


def _dot_tn(a, b):
    return lax.dot_general(a, b, (((0,), (0,)), ((), ())), preferred_element_type=F32)


class _Blocks:
    def __init__(self, n_pb, n_ss, bps):
        self.n_pb, self.n_ss, self.bps = n_pb, n_ss, bps
        self.n_blocks = n_pb + n_ss * bps
        self.n_seq = n_pb + n_ss

    def is_prompt(self, t):
        return t < self.n_pb

    def cond_row(self, t):
        return jnp.where(t < self.n_pb, 0, 1 + (t - self.n_pb) // self.bps)

    def seq(self, t):
        return jnp.where(t < self.n_pb, t, self.n_pb + (t - self.n_pb) // self.bps)

    def pos(self, t):
        return jnp.where(t < self.n_pb, 0, (t - self.n_pb) % self.bps)

    def last_pos(self, t):
        return jnp.where(t < self.n_pb, 0, self.bps - 1)

    def rev(self, t):
        u = t - self.n_pb
        return jnp.where(t < self.n_pb, t, self.n_pb + (u // self.bps) * self.bps + (self.bps - 1 - u % self.bps))


def _mod_kernel(c_ref, w_ref, b_ref, o_ref):
    s = _silu(c_ref[...]).astype(BF16)
    o_ref[...] = _dot(s, w_ref[...].astype(BF16)) + b_ref[...]


def _modulation(cond, w_mod, b_mod):
    depth = w_mod.shape[0]
    tn = 1536
    return pl.pallas_call(
        _mod_kernel,
        grid=(depth, 6 * D_MODEL // tn),
        in_specs=[
            pl.BlockSpec((COND_ROWS, D_MODEL), lambda l, j: (0, 0)),
            pl.BlockSpec((None, D_MODEL, tn), lambda l, j: (l, 0, j)),
            pl.BlockSpec((None, 1, tn), lambda l, j: (l, 0, j)),
        ],
        out_specs=pl.BlockSpec((None, COND_ROWS, tn), lambda l, j: (l, 0, j)),
        out_shape=jax.ShapeDtypeStruct((depth, COND_ROWS, 6 * D_MODEL), F32),
        compiler_params=_params(2),
        name="modulation",
    )(cond, w_mod, b_mod.reshape(depth, 1, 6 * D_MODEL))


def _in_kernel(x_ref, sh_ref, sc_ref, ng_ref, wg_ref, wr_ref, wl_ref, wa_ref, ba_ref,
               q_ref, k_ref, v_ref, g_ref, la_ref, cu_ref, dq_ref, dk_ref, dv_ref, gt_ref):
    x = x_ref[...]
    ms = jnp.mean(x * x, axis=-1, keepdims=True)
    h = x * lax.rsqrt(ms + EPS) * ng_ref[...]
    h = (h * (1.0 + sc_ref[...]) + sh_ref[...]).astype(BF16)

    def proj(w_ref, c0, width, out_ref, fn=None):
        step = 512
        for a in range(0, width, step):
            b = min(a + step, width)
            r = _dot(h, w_ref[:, c0 + a:c0 + b])
            out_ref[:, a:b] = r if fn is None else fn(r)

    proj(wg_ref, C_Q, GLA_QW, q_ref)
    proj(wg_ref, C_K, GLA_QW, k_ref)
    proj(wg_ref, C_V, GLA_VW, v_ref)
    proj(wg_ref, C_G, GLA_VW, g_ref)
    proj(wr_ref, 0, 2 * CONV_CH, cu_ref)
    proj(wr_ref, C_DQ - C_CU, DIFF_QW, dq_ref)
    proj(wr_ref, C_DK - C_CU, DIFF_QW, dk_ref)
    proj(wr_ref, C_DV - C_CU, DIFF_VW, dv_ref)
    proj(wr_ref, C_GT - C_CU, N_BRANCH * D_MODEL, gt_ref, lambda r: jax.nn.sigmoid(r).astype(BF16))
    lr = _dot(h, wl_ref[...]).astype(BF16)
    xa = _dot(lr, wa_ref[...]) + ba_ref[...]
    la_ref[...] = (jnp.minimum(xa, 0.0) - jnp.log1p(jnp.exp(-jnp.abs(xa)))) * (1.0 / GLA_TEMP)


def _in_projection(x, mod4, norm_g, w_gla, w_rest, w_lr, wa_p, ba_p, l, blk):
    n = x.shape[0]
    row = blk.cond_row

    def tok(width):
        return pl.BlockSpec((MM_BLOCK, width), lambda t: (t, 0))

    widths = (GLA_QW, GLA_QW, GLA_VW, GLA_VW, 2 * GLA_QW, 2 * CONV_CH, DIFF_QW, DIFF_QW, DIFF_VW, N_BRANCH * D_MODEL)
    return pl.pallas_call(
        _in_kernel,
        grid=(blk.n_blocks,),
        in_specs=[
            tok(D_MODEL),
            pl.BlockSpec((None, None, 1, D_MODEL), lambda t: (l, row(t), 0, 0)),
            pl.BlockSpec((None, None, 1, D_MODEL), lambda t: (l, row(t), 0, 1)),
            _resident((None, 1, D_MODEL), lambda t: (l, 0, 0)),
            _resident((None, D_MODEL, C_CU), lambda t: (l, 0, 0)),
            _resident((None, D_MODEL, C_LR - C_CU), lambda t: (l, 0, 0)),
            _resident((None, D_MODEL, LR_PAD), lambda t: (l, 0, 0)),
            _resident((None, LR_PAD, 2 * GLA_QW), lambda t: (l, 0, 0)),
            _resident((None, 1, 2 * GLA_QW), lambda t: (l, 0, 0)),
        ],
        out_specs=[tok(w) for w in widths],
        out_shape=[jax.ShapeDtypeStruct((n, w), F32) for w in widths[:-1]]
        + [jax.ShapeDtypeStruct((n, widths[-1]), BF16)],
        compiler_params=_params(1),
        name=f"in_projection_{l}",
    )(x, mod4, mod4, norm_g, w_gla, w_rest, w_lr, wa_p, ba_p)


def _gla_prepare(q_ref, k_ref, v_ref, la_ref, tri, causal, lo, order, mid, last):
    la = la_ref[...]
    hi = la.astype(BF16)
    r1 = la - hi.astype(F32)
    md = r1.astype(BF16)
    lw = (r1 - md.astype(F32)).astype(BF16)
    b = _dot(tri, hi) + _dot(tri, md) + _dot(tri, lw)
    q = q_ref[...] * (GLA_DK ** -0.5)
    k = k_ref[...]
    c = GLA_CHUNK
    pair = 2 * GLA_DK
    chunks = []
    for n in order:
        sl = slice(n * c, (n + 1) * c)
        bn = b[sl]
        bmid = bn[mid:mid + 1]
        blast = bn[last:last + 1]
        qn, kn = q[sl], k[sl]
        vn = v_ref[sl, :].astype(BF16)
        qe = qn * jnp.exp(bn)
        qt = qn * jnp.exp(bn - bmid)
        kt = (kn * jnp.exp(bmid - bn)).astype(BF16)
        ku = (kn * jnp.exp(blast - bn)).astype(BF16)
        decay = jnp.exp(jnp.broadcast_to(blast, (LANES, N_HEADS * GLA_DK)).T)
        qes, atts, incs = [], [], []
        for p in range(N_HEADS // 2):
            cs = slice(p * pair, (p + 1) * pair)
            qt_p, qe_p = qt[:, cs], qe[:, cs]
            lhs = jnp.concatenate([jnp.where(lo, qt_p, 0.0), jnp.where(lo, 0.0, qt_p)], axis=0).astype(BF16)
            att = _dot_nt(lhs, kt[:, cs])
            atts += [jnp.where(causal, att[0:c], 0.0).astype(BF16), jnp.where(causal, att[c:2 * c], 0.0).astype(BF16)]
            qes += [jnp.where(lo, qe_p, 0.0).astype(BF16), jnp.where(lo, 0.0, qe_p).astype(BF16)]
            uf = _dot_tn(ku[:, cs], vn[:, (2 * p) * GLA_DV:(2 * p + 2) * GLA_DV])
            incs += [uf[0:GLA_DK, 0:GLA_DV], uf[GLA_DK:2 * GLA_DK, GLA_DV:2 * GLA_DV]]
        chunks.append((sl, qes, atts, vn, decay, jnp.concatenate(incs, axis=0)))
    return chunks


def _gla_scan(chunks, s_scr, o_ref):
    pair = 2 * GLA_DK
    s = s_scr[...]
    for sl, qes, atts, vn, decay, inc in chunks:
        s_b16 = s.astype(BF16)
        for h in range(N_HEADS):
            s_p = s_b16[(h // 2) * pair:(h // 2 + 1) * pair]
            cs = slice(h * GLA_DV, (h + 1) * GLA_DV)
            o_ref[sl, cs] = _dot(qes[h], s_p) + _dot(atts[h], vn[:, cs])
        s = s * decay + inc
    s_scr[...] = s


def _gla_kernel(qf_ref, kf_ref, vf_ref, laf_ref, qb_ref, kb_ref, vb_ref, lab_ref, s0_ref,
                of_ref, ob_ref, sout_ref, sf_scr, sb_scr, *, blk):
    t = pl.program_id(0)
    pos = blk.pos(t)

    @pl.when(pos == 0)
    def _():
        sf_scr[...] = s0_ref[0]
        sb_scr[...] = s0_ref[1]

    tb, c = TOKEN_BLOCK, GLA_CHUNK
    shift = int(math.log2(c))
    row = lax.broadcasted_iota(jnp.int32, (tb, tb), 0)
    col = lax.broadcasted_iota(jnp.int32, (tb, tb), 1)
    same = jnp.right_shift(row, shift) == jnp.right_shift(col, shift)
    tri_f = jnp.where(same, jnp.where(col <= row, 1.0, 0.0), 0.0).astype(BF16)
    tri_b = jnp.where(same, jnp.where(col >= row, 1.0, 0.0), 0.0).astype(BF16)
    r64 = lax.broadcasted_iota(jnp.int32, (c, c), 0)
    c64 = lax.broadcasted_iota(jnp.int32, (c, c), 1)
    lo = lax.broadcasted_iota(jnp.int32, (c, 2 * GLA_DK), 1) < GLA_DK
    n_chunks = tb // c
    fwd = _gla_prepare(qf_ref, kf_ref, vf_ref, laf_ref, tri_f, c64 <= r64, lo,
                       tuple(range(n_chunks)), c // 2 - 1, c - 1)
    bwd = _gla_prepare(qb_ref, kb_ref, vb_ref, lab_ref, tri_b, c64 >= r64, lo,
                       tuple(reversed(range(n_chunks))), c // 2, 0)
    _gla_scan(fwd, sf_scr, of_ref)
    _gla_scan(bwd, sb_scr, ob_ref)

    @pl.when(pos == blk.last_pos(t))
    def _():
        sout_ref[0] = sf_scr[...]
        sout_ref[1] = sb_scr[...]


def _gla(q, k, v, la, s0, blk):
    n = q.shape[0]
    rev = blk.rev
    hk, hv = N_HEADS * GLA_DK, N_HEADS * GLA_DV

    def fwd(width, cb=0):
        return pl.BlockSpec((TOKEN_BLOCK, width), lambda t: (t, cb))

    def bwd(width, cb=0):
        return pl.BlockSpec((TOKEN_BLOCK, width), lambda t: (rev(t), cb))

    state_spec = pl.BlockSpec((None, 2, hk, GLA_DV), lambda t: (blk.seq(t), 0, 0, 0))
    return pl.pallas_call(
        functools.partial(_gla_kernel, blk=blk),
        grid=(blk.n_blocks,),
        in_specs=[fwd(hk), fwd(hk), fwd(hv), fwd(hk, 0), bwd(hk), bwd(hk), bwd(hv), bwd(hk, 1), state_spec],
        out_specs=[fwd(hv), bwd(hv), state_spec],
        out_shape=[jax.ShapeDtypeStruct((n, hv), F32), jax.ShapeDtypeStruct((n, hv), F32),
                   jax.ShapeDtypeStruct((blk.n_seq, 2, hk, GLA_DV), F32)],
        scratch_shapes=[pltpu.VMEM((hk, GLA_DV), F32), pltpu.VMEM((hk, GLA_DV), F32)],
        compiler_params=_params(1),
        name="gla_scan",
    )(q, k, v, la, q, k, v, la, s0)


def _conv_kernel(prev_ref, cur_ref, next_ref, w_ref, b_ref, lg_ref, lb_ref, o_ref, y_scr, ys_scr, *, blk):
    t = pl.program_id(0)
    pos = blk.pos(t)
    ch = CONV_CH

    def glu(u):
        return u[:, :ch] * jax.nn.sigmoid(u[:, ch:])

    y_scr[0:HALO, :] = jnp.where(pos == 0, 0.0, glu(prev_ref[...]))
    y_scr[HALO:HALO + TOKEN_BLOCK, :] = glu(cur_ref[...])
    y_scr[HALO + TOKEN_BLOCK:, :] = jnp.where(pos == blk.last_pos(t), 0.0, glu(next_ref[...]))

    for s in range(1, SUBLANES):
        ys_scr[s - 1] = y_scr[s:s + CONV_ROWS, :]

    rows = 32
    first = HALO - CONV_PAD
    for r0 in range(0, TOKEN_BLOCK, rows):
        acc = jnp.zeros((rows, ch), F32)
        for w in range(CONV_WIDTH):
            s, a = (first + w) % SUBLANES, r0 + (first + w) // SUBLANES * SUBLANES
            tap = y_scr[a:a + rows, :] if s == 0 else ys_scr[s - 1, a:a + rows, :]
            acc = acc + tap * w_ref[w:w + 1, :]
        y = acc + b_ref[...]
        mu = jnp.mean(y, axis=-1, keepdims=True)
        d = y - mu
        var = jnp.mean(d * d, axis=-1, keepdims=True)
        z = d * lax.rsqrt(var + EPS) * lg_ref[...] + lb_ref[...]
        o_ref[r0:r0 + rows, :] = _silu(z).astype(o_ref.dtype)


def _conv(cu, conv_w, conv_b, ln_g, ln_b, l, blk):
    n = cu.shape[0]
    hb = TOKEN_BLOCK // HALO
    n_halo = n // HALO
    vec = lambda: _resident((None, 1, CONV_CH), lambda t: (l, 0, 0))
    return pl.pallas_call(
        functools.partial(_conv_kernel, blk=blk),
        grid=(blk.n_blocks,),
        in_specs=[
            pl.BlockSpec((HALO, 2 * CONV_CH), lambda t: (jnp.maximum(t * hb - 1, 0), 0)),
            pl.BlockSpec((TOKEN_BLOCK, 2 * CONV_CH), lambda t: (t, 0)),
            pl.BlockSpec((HALO, 2 * CONV_CH), lambda t: (jnp.minimum((t + 1) * hb, n_halo - 1), 0)),
            _resident((None, CONV_WIDTH, CONV_CH), lambda t: (l, 0, 0)),
            vec(), vec(), vec(),
        ],
        out_specs=pl.BlockSpec((TOKEN_BLOCK, CONV_CH), lambda t: (t, 0)),
        out_shape=jax.ShapeDtypeStruct((n, CONV_CH), BF16),
        scratch_shapes=[pltpu.VMEM((TOKEN_BLOCK + 2 * HALO, CONV_CH), F32),
                        pltpu.VMEM((SUBLANES - 1, CONV_ROWS, CONV_CH), F32)],
        compiler_params=_params(1),
        name=f"conv_module_{l}",
    )(cu, cu, cu, conv_w, conv_b, ln_g, ln_b)


def _rope(x, cos, sin_signed, first_half):
    partner = jnp.where(first_half, pltpu.roll(x, LANES - ROPE_FREQS, 1), pltpu.roll(x, ROPE_FREQS, 1))
    return x * cos + partner * sin_signed


def _diff_kernel(*refs, rope, lam_init, n_self, n_cache, tq):
    if rope:
        (q_ref, k_ref, v_ref, ck_ref, cv_ref, cosq_ref, sinq_ref, cosk_ref, sink_ref, lam_ref, g_ref,
         o_ref, k_scr, v_scr) = refs
    else:
        q_ref, k_ref, v_ref, lam_ref, g_ref, o_ref, k_scr, v_scr = refs
    slab = 2 * DIFF_DH
    lane = lax.broadcasted_iota(jnp.int32, (1, slab), 1)
    first_half = (lane % (2 * ROPE_FREQS)) < ROPE_FREQS
    lo = lane < DIFF_DH

    @pl.when(pl.program_id(1) == 0)
    def _():
        rows = 256
        for r0 in range(0, n_self, rows):
            rs = slice(r0, r0 + rows)
            for h in range(N_HEADS):
                cs = slice(h * slab, (h + 1) * slab)
                kx = k_ref[rs, cs]
                if rope:
                    kx = _rope(kx, cosk_ref[rs, :], sink_ref[rs, :], first_half)
                k_scr[rs, cs] = kx.astype(BF16)
            for h in range(N_HEADS):
                v_scr[rs, (2 * h) * DIFF_DV:(2 * h + 1) * DIFF_DV] = v_ref[rs, h * DIFF_DV:(h + 1) * DIFF_DV].astype(BF16)
                v_scr[rs, (2 * h + 1) * DIFF_DV:(2 * h + 2) * DIFF_DV] = jnp.ones((rows, DIFF_DV), BF16)
        if n_cache:
            cs_rows = slice(n_self, n_self + n_cache)
            k_scr[cs_rows, :] = ck_ref[...].astype(BF16)
            for h in range(N_HEADS):
                v_scr[cs_rows, (2 * h) * DIFF_DV:(2 * h + 1) * DIFF_DV] = cv_ref[:, h * DIFF_DV:(h + 1) * DIFF_DV].astype(BF16)
                v_scr[cs_rows, (2 * h + 1) * DIFF_DV:(2 * h + 2) * DIFF_DV] = jnp.ones((n_cache, DIFF_DV), BF16)

    lv = lam_ref[...]
    lam = (jnp.exp(jnp.sum(lv[0:1] * lv[1:2], axis=-1, keepdims=True))
           - jnp.exp(jnp.sum(lv[2:3] * lv[3:4], axis=-1, keepdims=True)) + lam_init)
    scale = DIFF_DH ** -0.5
    for h in range(N_HEADS):
        cs = slice(h * slab, (h + 1) * slab)
        qh = q_ref[:, cs]
        if rope:
            qh = _rope(qh, cosq_ref[...], sinq_ref[...], first_half)
        qh = qh * scale
        lhs = jnp.concatenate([jnp.where(lo, qh, 0.0), jnp.where(lo, 0.0, qh)], axis=0).astype(BF16)
        s = _dot_nt(lhs, k_scr[:, cs])
        p = jnp.exp(s - jnp.max(s, axis=-1, keepdims=True)).astype(BF16)
        pv = _dot(p, v_scr[:, (2 * h) * DIFF_DV:(2 * h + 2) * DIFF_DV])
        o = (pv[0:tq, 0:DIFF_DV] * (1.0 / pv[0:tq, DIFF_DV:])
             - pv[tq:, 0:DIFF_DV] * (lam / pv[tq:, DIFF_DV:]))
        ms = jnp.mean(o * o, axis=-1, keepdims=True)
        y = o * lax.rsqrt(ms + EPS) * g_ref[...] * (1.0 - lam_init)
        o_ref[:, h * DIFF_DV:(h + 1) * DIFF_DV] = y.astype(o_ref.dtype)


def _diff_attention(dq, dk, dv, cache_k, cache_v, rope_tabs, diff_lambda, subln_g, l, blk):
    n = dq.shape[0]
    lam_init = 0.8 - 0.6 * math.exp(-0.3 * l)
    n_p = blk.n_pb * TOKEN_BLOCK
    ls = blk.bps * TOKEN_BLOCK
    lam_spec = lambda: pl.BlockSpec((None, 4, DIFF_DH), lambda s, i: (l, 0, 0))
    g_spec = lambda: pl.BlockSpec((None, 1, DIFF_DV), lambda s, i: (l, 0, 0))

    o_prompt = pl.pallas_call(
        functools.partial(_diff_kernel, rope=False, lam_init=lam_init, n_self=TOKEN_BLOCK, n_cache=0, tq=TOKEN_BLOCK),
        grid=(blk.n_pb, 1),
        in_specs=[pl.BlockSpec((TOKEN_BLOCK, DIFF_QW), lambda s, i: (s, 0)),
                  pl.BlockSpec((TOKEN_BLOCK, DIFF_QW), lambda s, i: (s, 0)),
                  pl.BlockSpec((TOKEN_BLOCK, DIFF_VW), lambda s, i: (s, 0)),
                  lam_spec(), g_spec()],
        out_specs=pl.BlockSpec((TOKEN_BLOCK, DIFF_VW), lambda s, i: (s, 0)),
        out_shape=jax.ShapeDtypeStruct((n_p, DIFF_VW), BF16),
        scratch_shapes=[pltpu.VMEM((TOKEN_BLOCK, DIFF_QW), BF16), pltpu.VMEM((TOKEN_BLOCK, 2 * DIFF_VW), BF16)],
        compiler_params=_params(2),
        name=f"diff_attn_prompt_{l}",
    )(dq, dk, dv, diff_lambda, subln_g)

    tq = TOKEN_BLOCK
    n_cache = cache_k.shape[2]
    cos_t, sin_t = rope_tabs
    seq0 = n_p // ls
    assert seq0 * ls == n_p
    q0 = n_p // tq
    o_all = pl.pallas_call(
        functools.partial(_diff_kernel, rope=True, lam_init=lam_init, n_self=ls, n_cache=n_cache, tq=tq),
        grid=(blk.n_ss, ls // tq),
        in_specs=[pl.BlockSpec((tq, DIFF_QW), lambda s, i: (q0 + s * (ls // tq) + i, 0)),
                  pl.BlockSpec((ls, DIFF_QW), lambda s, i: (seq0 + s, 0)),
                  pl.BlockSpec((ls, DIFF_VW), lambda s, i: (seq0 + s, 0)),
                  pl.BlockSpec((None, None, n_cache, DIFF_QW), lambda s, i: (s, l, 0, 0)),
                  pl.BlockSpec((None, None, n_cache, DIFF_VW), lambda s, i: (s, l, 0, 0)),
                  pl.BlockSpec((tq, LANES), lambda s, i: (i, 0)),
                  pl.BlockSpec((tq, LANES), lambda s, i: (i, 0)),
                  _resident((ls, LANES), lambda s, i: (0, 0)),
                  _resident((ls, LANES), lambda s, i: (0, 0)),
                  lam_spec(), g_spec()],
        out_specs=pl.BlockSpec((tq, DIFF_VW), lambda s, i: (s * (ls // tq) + i, 0)),
        out_shape=jax.ShapeDtypeStruct((n - n_p, DIFF_VW), BF16),
        scratch_shapes=[pltpu.VMEM((ls + n_cache, DIFF_QW), BF16), pltpu.VMEM((ls + n_cache, 2 * DIFF_VW), BF16)],
        compiler_params=_params(2),
        name=f"diff_attn_latent_{l}",
    )(dq, dk, dv, cache_k, cache_v, cos_t, sin_t, cos_t, sin_t, diff_lambda, subln_g)
    return o_prompt, o_all


def _merge_kernel(x_ref, of_ref, ob_ref, g_ref, oc_ref, odp_ref, ods_ref, gt_ref, g1_ref, gng_ref, wb_ref, wo_ref,
                  out_ref, *, n_prompt_blocks):
    o = of_ref[...] + ob_ref[...]
    parts = []
    for h in range(N_HEADS):
        cs = slice(h * GLA_DV, (h + 1) * GLA_DV)
        oh = o[:, cs]
        ms = jnp.mean(oh * oh, axis=-1, keepdims=True)
        parts.append((oh * lax.rsqrt(ms + EPS) * gng_ref[...] * _silu(g_ref[:, cs])).astype(BF16))
    o_gla = jnp.concatenate(parts, axis=1)
    o_conv = oc_ref[...].astype(BF16)
    o_diff = jnp.where(pl.program_id(0) < n_prompt_blocks, odp_ref[...], ods_ref[...]).astype(BF16)
    b1 = GLA_VW
    b2 = b1 + CONV_CH
    b3 = b2 + DIFF_VW
    half = D_MODEL // 2
    merged = []
    for c0 in range(0, D_MODEL, half):
        cs = slice(c0, c0 + half)
        m = gt_ref[:, c0:c0 + half] * _dot(o_gla, wb_ref[0:b1, cs])
        m = m + gt_ref[:, D_MODEL + c0:D_MODEL + c0 + half] * _dot(o_conv, wb_ref[b1:b2, cs])
        m = m + gt_ref[:, 2 * D_MODEL + c0:2 * D_MODEL + c0 + half] * _dot(o_diff, wb_ref[b2:b3, cs])
        merged.append(m.astype(BF16))
    merged = jnp.concatenate(merged, axis=1)
    out_ref[...] = x_ref[...] + g1_ref[...] * _dot(merged, wo_ref[...])


def _merge(x, o_f, o_b, g, o_conv, o_diff_p, o_diff_s, gates, mod4, gla_norm_g, wb, wo, l, blk):
    n = x.shape[0]
    npb = blk.n_pb

    def tok(width):
        return pl.BlockSpec((MM_BLOCK, width), lambda t: (t, 0))

    return pl.pallas_call(
        functools.partial(_merge_kernel, n_prompt_blocks=npb),
        grid=(blk.n_blocks,),
        in_specs=[
            tok(D_MODEL), tok(GLA_VW), tok(GLA_VW), tok(GLA_VW), tok(CONV_CH),
            pl.BlockSpec((MM_BLOCK, DIFF_VW), lambda t: (jnp.minimum(t, npb - 1), 0)),
            pl.BlockSpec((MM_BLOCK, DIFF_VW), lambda t: (jnp.maximum(t - npb, 0), 0)),
            tok(N_BRANCH * D_MODEL),
            pl.BlockSpec((None, None, 1, D_MODEL), lambda t: (l, blk.cond_row(t), 0, 2)),
            _resident((None, 1, GLA_DV), lambda t: (l, 0, 0)),
            _resident((None, GLA_VW + CONV_CH + DIFF_VW, D_MODEL), lambda t: (l, 0, 0)),
            _resident((None, D_MODEL, D_MODEL), lambda t: (l, 0, 0)),
        ],
        out_specs=tok(D_MODEL),
        out_shape=jax.ShapeDtypeStruct((n, D_MODEL), F32),
        compiler_params=_params(1),
        name=f"merge_out_{l}",
    )(x, o_f, o_b, g, o_conv, o_diff_p, o_diff_s, gates, mod4, gla_norm_g, wb, wo)


def _ffn_kernel(x_ref, sh2_ref, sc2_ref, g2_ref, n2g_ref, fng_ref, wfi_ref, wfo_ref, out_ref, acc_ref, *, final):
    x1 = x_ref[...]
    ms = jnp.mean(x1 * x1, axis=-1, keepdims=True)
    h2 = x1 * lax.rsqrt(ms + EPS) * n2g_ref[...]
    h2 = (h2 * (1.0 + sc2_ref[...]) + sh2_ref[...]).astype(BF16)
    for i, c0 in enumerate(range(0, FFN_HIDDEN, FFN_CHUNK)):
        a = _dot(h2, wfi_ref[:, c0:c0 + FFN_CHUNK])
        b = _dot(h2, wfi_ref[:, FFN_HIDDEN + c0:FFN_HIDDEN + c0 + FFN_CHUNK])
        u = (_silu(a) * b).astype(BF16)
        y = _dot(u, wfo_ref[c0:c0 + FFN_CHUNK, :])
        if i == 0:
            acc_ref[...] = y
        else:
            acc_ref[...] += y
    x2 = x_ref[...] + g2_ref[...] * acc_ref[...]
    if final:
        ms = jnp.mean(x2 * x2, axis=-1, keepdims=True)
        out_ref[...] = x2 * lax.rsqrt(ms + EPS) * fng_ref[...]
    else:
        out_ref[...] = x2


def _ffn(x, mod4, norm2_g, final_g, wfi, wfo, l, blk, final):
    n = x.shape[0]

    def mod(j):
        return pl.BlockSpec((None, None, 1, D_MODEL), lambda t: (l, blk.cond_row(t), 0, j))

    return pl.pallas_call(
        functools.partial(_ffn_kernel, final=final),
        grid=(blk.n_blocks,),
        in_specs=[
            pl.BlockSpec((FFN_BLOCK, D_MODEL), lambda t: (t, 0)),
            mod(3), mod(4), mod(5),
            _resident((None, 1, D_MODEL), lambda t: (l, 0, 0)),
            _resident((1, D_MODEL), lambda t: (0, 0)),
            _resident((None, D_MODEL, 2 * FFN_HIDDEN), lambda t: (l, 0, 0)),
            _resident((None, FFN_HIDDEN, D_MODEL), lambda t: (l, 0, 0)),
        ],
        out_specs=pl.BlockSpec((FFN_BLOCK, D_MODEL), lambda t: (t, 0)),
        out_shape=jax.ShapeDtypeStruct((n, D_MODEL), F32),
        scratch_shapes=[pltpu.VMEM((FFN_BLOCK, D_MODEL), F32)],
        compiler_params=_params(1),
        name=f"ffn_{l}",
    )(x, mod4, mod4, mod4, norm2_g, final_g, wfi, wfo)


def _rope_tables(n_tokens):
    rows = n_tokens // GRID_W
    row = jnp.repeat(jnp.arange(rows, dtype=F32), GRID_W)
    col = jnp.tile(jnp.arange(GRID_W, dtype=F32), rows)
    inv = ROPE_BASE ** (-jnp.arange(ROPE_FREQS, dtype=F32) / ROPE_FREQS)
    ang = jnp.stack([row[:, None] * inv, col[:, None] * inv], axis=1)
    cos, sin = jnp.cos(ang), jnp.sin(ang)
    cos_g = jnp.concatenate([cos, cos], axis=-1).reshape(n_tokens, 4 * ROPE_FREQS)
    sin_g = jnp.concatenate([-sin, sin], axis=-1).reshape(n_tokens, 4 * ROPE_FREQS)
    return jnp.tile(cos_g, (1, 2)), jnp.tile(sin_g, (1, 2))


def kernel(x_prompt, x_sample, cache_diff_k, cache_diff_v, state_gla, c, c_ctx, w_mod, b_mod, norm1_g, norm2_g,
           w_in, gla_w_a2, gla_b_a, gla_norm_g, conv_w, conv_b, conv_ln_g, conv_ln_b, diff_lambda, diff_subln_g,
           w_branch, w_out, w_ffn_in, w_ffn_out, final_norm_g):
    batch, seq, d = x_prompt.shape
    dec_batch, dec_seq, _ = x_sample.shape
    depth = w_mod.shape[0]
    assert d == D_MODEL and seq == TOKEN_BLOCK and dec_seq % TOKEN_BLOCK == 0 and dec_batch + 1 <= COND_ROWS
    blk = _Blocks(batch, dec_batch, dec_seq // TOKEN_BLOCK)
    n_p = batch * seq
    assert n_p % FFN_BLOCK == 0 and dec_seq % FFN_BLOCK == 0 and FFN_BLOCK % MM_BLOCK == 0
    blk_mm = _Blocks(n_p // MM_BLOCK, dec_batch, dec_seq // MM_BLOCK)
    blk_ffn = _Blocks(n_p // FFN_BLOCK, dec_batch, dec_seq // FFN_BLOCK)

    x = jnp.concatenate([x_prompt.reshape(n_p, d), x_sample.reshape(dec_batch * dec_seq, d)], axis=0)
    cond = jnp.concatenate([c_ctx[None, :], c, jnp.zeros((COND_ROWS - 1 - dec_batch, d), F32)], axis=0)
    assert w_in.shape[-1] == OFFS[10]
    w_gla = w_in[..., OFFS[0]:OFFS[4]].astype(BF16)
    w_rest = w_in[..., OFFS[5]:OFFS[10]].astype(BF16)
    w_lr = jnp.pad(w_in[..., OFFS[4]:OFFS[5]], ((0, 0), (0, 0), (0, LR_PAD - 2 * GLA_RANK))).astype(BF16)
    wa_p = jnp.zeros((depth, LR_PAD, 2 * GLA_QW), F32)
    wa_p = wa_p.at[:, 0:GLA_RANK, 0:GLA_QW].set(gla_w_a2[:, 0])
    wa_p = wa_p.at[:, GLA_RANK:2 * GLA_RANK, GLA_QW:].set(gla_w_a2[:, 1]).astype(BF16)
    ba_p = gla_b_a.reshape(depth, 1, 2 * GLA_QW)
    wb = w_branch.astype(BF16)
    wo = w_out.astype(BF16)
    wfi = w_ffn_in.astype(BF16)
    wfo = w_ffn_out.astype(BF16)
    vec = lambda a: a.reshape(depth, 1, a.shape[-1])
    rope_tabs = _rope_tables(dec_seq)
    ck = cache_diff_k.reshape(dec_batch, depth, -1, DIFF_QW)
    cv = cache_diff_v.reshape(dec_batch, depth, -1, DIFF_VW)
    hk = N_HEADS * GLA_DK
    s0_all = jnp.concatenate(
        [jnp.zeros((batch, depth, 2, hk, GLA_DV), F32), state_gla.reshape(dec_batch, depth, 2, hk, GLA_DV)], axis=0)

    mod4 = _modulation(cond, w_mod, b_mod).reshape(depth, COND_ROWS, 1, 6 * d)

    new_k, new_v, new_s = [], [], []
    for l in range(depth):
        q, k, v, g, la, cu, dq, dk, dv, gates = _in_projection(x, mod4, vec(norm1_g), w_gla, w_rest, w_lr, wa_p, ba_p, l, blk_mm)
        o_f, o_b, s_out = _gla(q, k, v, la, s0_all[:, l], blk)
        o_conv = _conv(cu, conv_w, vec(conv_b), vec(conv_ln_g), vec(conv_ln_b), l, blk)
        o_diff_p, o_diff_s = _diff_attention(dq, dk, dv, ck, cv, rope_tabs, diff_lambda, vec(diff_subln_g), l, blk)
        x = _merge(x, o_f, o_b, g, o_conv, o_diff_p, o_diff_s, gates, mod4, vec(gla_norm_g), wb, wo, l, blk_mm)
        x = _ffn(x, mod4, vec(norm2_g), final_norm_g.reshape(1, d), wfi, wfo, l, blk_ffn, final=(l == depth - 1))
        new_k.append(dk[:n_p].reshape(batch, seq, N_HEADS, 2, DIFF_DH))
        new_v.append(dv[:n_p].reshape(batch, seq, N_HEADS, DIFF_DV))
        new_s.append(s_out[:batch].reshape(batch, 2, N_HEADS, GLA_DK, GLA_DV))
    y_prompt = x[:n_p].reshape(batch, seq, d)
    y_sample = x[n_p:].reshape(dec_batch, dec_seq, d)
    return (y_prompt, y_sample, jnp.stack(new_k, axis=1), jnp.stack(new_v, axis=1), jnp.stack(new_s, axis=1))
```
